```python
import jax, jax.numpy as jnp
from jax import lax
import numpy as np

D_MODEL = 1024
BATCH = 32
SEQ = 2048
DEPTH = 4
DEC_BATCH = 32
DEC_SEQ = 32
PAST_LEN = 1024

CHUNK = 64
EPS = 1e-6
F32 = jnp.float32

S5_WIDTH = D_MODEL // 4
S5_GROUP_CH = 16
S5_GROUPS = S5_WIDTH // S5_GROUP_CH
S5_STATE = 64
RET_HEADS = 4
RET_DK = 64
RET_DV = 64
RET_QK = RET_HEADS * RET_DK
RET_WIDTH = RET_HEADS * RET_DV
ROPE_BASE = 10000.0
SSD_WIDTH = D_MODEL // 2
SSD_HEADDIM = 64
SSD_HEADS = SSD_WIDTH // SSD_HEADDIM
SSD_GROUPS = 2
SSD_DSTATE = 128
SSD_CONV = 4
SSD_CONV_DIM = SSD_WIDTH + 2 * SSD_GROUPS * SSD_DSTATE
MIX_WIDTH = S5_WIDTH + RET_WIDTH + SSD_WIDTH
IN_SIZES = (S5_WIDTH, RET_QK, RET_QK, RET_WIDTH, RET_WIDTH, SSD_WIDTH, SSD_CONV_DIM, SSD_HEADS)
IN_WIDTH = S5_WIDTH + 2 * RET_QK + 2 * RET_WIDTH + SSD_WIDTH + SSD_CONV_DIM + SSD_HEADS
IN_OFFSETS = [S5_WIDTH,
              S5_WIDTH + RET_QK,
              S5_WIDTH + 2 * RET_QK,
              S5_WIDTH + 2 * RET_QK + RET_WIDTH,
              S5_WIDTH + 2 * RET_QK + 2 * RET_WIDTH,
              S5_WIDTH + 2 * RET_QK + 2 * RET_WIDTH + SSD_WIDTH,
              S5_WIDTH + 2 * RET_QK + 2 * RET_WIDTH + SSD_WIDTH + SSD_CONV_DIM]
D_FF = -(-8 * D_MODEL // (3 * 256)) * 256

kernel_name = 'hymba_s5_retnet_ssd_streaming_step'


def rmsnorm(x, g):
    xf = x.astype(F32)
    y = xf * lax.rsqrt(jnp.mean(xf * xf, axis=-1, keepdims=True) + EPS)
    return (y * g.astype(F32)).astype(x.dtype)


def s5_mixer(u, h0_re, h0_im, lam_re, lam_im, log_dt, b_re, b_im, c_re, c_im, d_skip, w_glu, b_glu):
    bsz, L, _ = u.shape
    uf = u.astype(F32)
    ug = uf.reshape(bsz, L, S5_GROUPS, S5_GROUP_CH)
    dt = jnp.exp(log_dt.astype(F32))[:, None]
    lr, li = lam_re.astype(F32), lam_im.astype(F32)
    mag = jnp.exp(lr * dt)
    a_re = mag * jnp.cos(li * dt)
    a_im = mag * jnp.sin(li * dt)
    den = lr * lr + li * li
    nr, ni = a_re - 1.0, a_im
    k_re = (nr * lr + ni * li) / den
    k_im = (ni * lr - nr * li) / den
    br, bi = b_re.astype(F32), b_im.astype(F32)
    bb_re = k_re[..., None] * br - k_im[..., None] * bi
    bb_im = k_re[..., None] * bi + k_im[..., None] * br
    bu_re = jnp.einsum('gpc,blgc->blgp', bb_re, ug)
    bu_im = jnp.einsum('gpc,blgc->blgp', bb_im, ug)
    ar = jnp.broadcast_to(a_re, bu_re.shape)
    ai = jnp.broadcast_to(a_im, bu_re.shape)

    def combine(e1, e2):
        a1r, a1i, b1r, b1i = e1
        a2r, a2i, b2r, b2i = e2
        return (a2r * a1r - a2i * a1i, a2r * a1i + a2i * a1r,
                a2r * b1r - a2i * b1i + b2r, a2r * b1i + a2i * b1r + b2i)

    pr, pi, sr, si = lax.associative_scan(combine, (ar, ai, bu_re, bu_im), axis=1)
    h0r = h0_re.astype(F32)[:, None]
    h0i = h0_im.astype(F32)[:, None]
    sr = sr + pr * h0r - pi * h0i
    si = si + pr * h0i + pi * h0r
    y = (jnp.einsum('gcp,blgp->blgc', c_re.astype(F32), sr)
         - jnp.einsum('gcp,blgp->blgc', c_im.astype(F32), si))
    y = y.reshape(bsz, L, S5_WIDTH) + d_skip.astype(F32) * uf
    z = jax.nn.gelu(y)
    out = z * jax.nn.sigmoid(z @ w_glu.astype(F32) + b_glu.astype(F32))
    return out, sr[:, -1], si[:, -1]


def rotary(x, pos):
    half = x.shape[-1] // 2
    inv_freq = ROPE_BASE ** (-jnp.arange(half, dtype=F32) / half)
    ang = pos.astype(F32)[:, None] * inv_freq[None, :]
    cos = jnp.cos(ang)[None, :, None, :]
    sin = jnp.sin(ang)[None, :, None, :]
    x1, x2 = x[..., :half], x[..., half:]
    return jnp.concatenate([x1 * cos - x2 * sin, x1 * sin + x2 * cos], axis=-1)


def retention(q, k, v, g, s0, t0):
    bsz, L = q.shape[:2]
    lc = min(CHUNK, L)
    nc = L // lc
    pos = t0 + jnp.arange(L)
    q = rotary(q.astype(F32), pos)
    k = rotary(k.astype(F32), pos) * (RET_DK ** -0.5)
    v = v.astype(F32)
    log_g = jnp.log(1.0 - 2.0 ** (-5.0 - jnp.arange(RET_HEADS, dtype=F32)))
    idx = jnp.arange(lc, dtype=F32)
    rel = idx[:, None] - idx[None, :]
    decay = jnp.where(rel >= 0, jnp.exp(log_g[:, None, None] * jnp.maximum(rel, 0.0)), 0.0)
    qc = q.reshape(bsz, nc, lc, RET_HEADS, RET_DK)
    kc = k.reshape(bsz, nc, lc, RET_HEADS, RET_DK)
    vc = v.reshape(bsz, nc, lc, RET_HEADS, RET_DV)
    scores = jnp.einsum('bcihd,bcjhd->bchij', qc, kc) * decay
    o_intra = jnp.einsum('bchij,bcjhv->bcihv', scores, vc)
    k_w = jnp.exp(log_g[None, :] * (lc - 1.0 - idx)[:, None])
    kv = jnp.einsum('bcjhd,bcjhv,jh->bchdv', kc, vc, k_w)
    chunk_decay = jnp.exp(log_g * lc)[None, :, None, None]

    def step(s, kv_c):
        return chunk_decay * s + kv_c, s

    s_fin, s_prev = lax.scan(step, s0.astype(F32), jnp.moveaxis(kv, 1, 0))
    s_prev = jnp.moveaxis(s_prev, 0, 1)
    q_w = jnp.exp(log_g[None, :] * (idx + 1.0)[:, None])
    o_inter = jnp.einsum('bcihd,bchdv,ih->bcihv', qc, s_prev, q_w)
    o = (o_intra + o_inter).reshape(bsz, L, RET_HEADS, RET_DV)
    mu = jnp.mean(o, axis=-1, keepdims=True)
    var = jnp.mean((o - mu) ** 2, axis=-1, keepdims=True)
    o = (o - mu) * lax.rsqrt(var + EPS)
    out = jax.nn.silu(g.astype(F32)) * o
    return out.reshape(bsz, L, RET_WIDTH), s_fin


def causal_conv(xbc, conv_state, w, b):
    L = xbc.shape[1]
    xp = jnp.concatenate([conv_state.astype(F32), xbc.astype(F32)], axis=1)
    out = b.astype(F32)
    for i in range(SSD_CONV):
        out = out + xp[:, i:i + L] * w[i].astype(F32)
    return jax.nn.silu(out), xp[:, -(SSD_CONV - 1):]


def ssd_scan(x, dt, bm, cm, s0, a_log, d_skip):
    bsz, L = x.shape[:2]
    lc = min(CHUNK, L)
    nc = L // lc
    hg = SSD_HEADS // SSD_GROUPS
    a = -jnp.exp(a_log.astype(F32)).reshape(SSD_GROUPS, hg)
    dtc = dt.reshape(bsz, nc, lc, SSD_GROUPS, hg)
    cs = jnp.cumsum(dtc * a, axis=2)
    xc = x.astype(F32).reshape(bsz, nc, lc, SSD_GROUPS, hg, SSD_HEADDIM)
    xdt = xc * dtc[..., None]
    bc = bm.astype(F32).reshape(bsz, nc, lc, SSD_GROUPS, SSD_DSTATE)
    cc = cm.astype(F32).reshape(bsz, nc, lc, SSD_GROUPS, SSD_DSTATE)
    idx = jnp.arange(lc)
    causal = (idx[:, None] >= idx[None, :])[:, :, None, None]
    seg = cs[:, :, :, None] - cs[:, :, None, :]
    lmat = jnp.exp(jnp.where(causal, seg, -jnp.inf))
    cb = jnp.einsum('bcign,bcjgn->bcijg', cc, bc)
    y_intra = jnp.einsum('bcijg,bcijgh,bcjghp->bcighp', cb, lmat, xdt)
    last = cs[:, :, -1]
    w_end = jnp.exp(last[:, :, None] - cs)
    chunk_states = jnp.einsum('bcjgn,bcjgh,bcjghp->bcghpn', bc, w_end, xdt)
    chunk_decay = jnp.exp(last)

    def step(s, inp):
        st_c, dec_c = inp
        return dec_c[..., None, None] * s + st_c, s

    s_init = s0.astype(F32).reshape(bsz, SSD_GROUPS, hg, SSD_HEADDIM, SSD_DSTATE)
    s_fin, s_prev = lax.scan(step, s_init, (jnp.moveaxis(chunk_states, 1, 0), jnp.moveaxis(chunk_decay, 1, 0)))
    s_prev = jnp.moveaxis(s_prev, 0, 1)
    y_inter = jnp.einsum('bcign,bcghpn,bcigh->bcighp', cc, s_prev, jnp.exp(cs))
    y = y_intra + y_inter + d_skip.astype(F32).reshape(SSD_GROUPS, hg)[:, :, None] * xc
    return y.reshape(bsz, L, SSD_WIDTH), s_fin.reshape(bsz, SSD_HEADS, SSD_HEADDIM, SSD_DSTATE)


def trunk_layer(x, st, p, l, t0):
    s5_re0, s5_im0, ret0, ssd0, conv0 = st
    bsz, L, _ = x.shape
    h = rmsnorm(x, p['norm1_g'][l])
    proj = h @ p['w_in'][l]
    u, q, k, v, g, z, xbc, dt_raw = jnp.split(proj, IN_OFFSETS, axis=-1)
    ya, s5_re, s5_im = s5_mixer(u, s5_re0, s5_im0, p['s5_lam_re'][l], p['s5_lam_im'][l], p['s5_log_dt'][l],
                                p['s5_b_re'][l], p['s5_b_im'][l], p['s5_c_re'][l], p['s5_c_im'][l],
                                p['s5_d'][l], p['s5_w_glu'][l], p['s5_b_glu'][l])
    yb, ret_s = retention(q.reshape(bsz, L, RET_HEADS, RET_DK), k.reshape(bsz, L, RET_HEADS, RET_DK),
                          v.reshape(bsz, L, RET_HEADS, RET_DV), g.reshape(bsz, L, RET_HEADS, RET_DV), ret0, t0)
    xbc_c, conv_s = causal_conv(xbc, conv0, p['ssd_conv_w'][l], p['ssd_conv_b'][l])
    xs, bm, cm = jnp.split(xbc_c, [SSD_WIDTH, SSD_WIDTH + SSD_GROUPS * SSD_DSTATE], axis=-1)
    dt = jax.nn.softplus(dt_raw.astype(F32) + p['ssd_dt_bias'][l].astype(F32))
    ys, ssd_s = ssd_scan(xs, dt, bm, cm, ssd0, p['ssd_a_log'][l], p['ssd_d'][l])
    yc = rmsnorm(ys * jax.nn.silu(z.astype(F32)), p['ssd_norm_g'][l])
    mix = jnp.concatenate([ya, yb, yc], axis=-1).astype(x.dtype)
    x = x + mix @ p['w_out'][l]
    h2 = rmsnorm(x, p['norm2_g'][l])
    x = x + (jax.nn.silu(h2 @ p['w_gate'][l]) * (h2 @ p['w_up'][l])) @ p['w_down'][l]
    return x, (s5_re, s5_im, ret_s, ssd_s, conv_s)


def zero_state(b):
    return (jnp.zeros((b, S5_GROUPS, S5_STATE), F32),
            jnp.zeros((b, S5_GROUPS, S5_STATE), F32),
            jnp.zeros((b, RET_HEADS, RET_DK, RET_DV), F32),
            jnp.zeros((b, SSD_HEADS, SSD_HEADDIM, SSD_DSTATE), F32),
            jnp.zeros((b, SSD_CONV - 1, SSD_CONV_DIM), F32))


def setup_inputs(seed: int = 0) -> dict:
    key = jax.random.key(seed)
    ks = iter(list(jax.random.split(key, 40)))

    def nrm(shape, scale):
        return scale * jax.random.normal(next(ks), shape, F32)

    def unif(shape, lo, hi):
        return jax.random.uniform(next(ks), shape, F32, lo, hi)

    x_prompt = nrm((BATCH, SEQ, D_MODEL), 1.0)
    x_sample = nrm((DEC_BATCH, DEC_SEQ, D_MODEL), 1.0)
    state_s5_re = nrm((DEPTH, DEC_BATCH, S5_GROUPS, S5_STATE), 0.1)
    state_s5_im = nrm((DEPTH, DEC_BATCH, S5_GROUPS, S5_STATE), 0.1)
    state_ret = nrm((DEPTH, DEC_BATCH, RET_HEADS, RET_DK, RET_DV), 1.0)
    state_ssd = nrm((DEPTH, DEC_BATCH, SSD_HEADS, SSD_HEADDIM, SSD_DSTATE), 0.1)
    cache_ssd_conv = nrm((DEPTH, DEC_BATCH, SSD_CONV - 1, SSD_CONV_DIM), 1.0)
    norm1_g = 1.0 + nrm((DEPTH, D_MODEL), 0.02)
    w_in = nrm((DEPTH, D_MODEL, IN_WIDTH), D_MODEL ** -0.5)
    s5_lam_re = -0.5 + nrm((DEPTH, S5_GROUPS, S5_STATE), 0.01)
    s5_lam_im = jnp.pi * jnp.arange(S5_STATE, dtype=F32) + nrm((DEPTH, S5_GROUPS, S5_STATE), 0.01)
    s5_log_dt = unif((DEPTH, S5_GROUPS), float(np.log(1e-3)), float(np.log(1e-1)))
    s5_b_re = nrm((DEPTH, S5_GROUPS, S5_STATE, S5_GROUP_CH), (2.0 * S5_GROUP_CH) ** -0.5)
    s5_b_im = nrm((DEPTH, S5_GROUPS, S5_STATE, S5_GROUP_CH), (2.0 * S5_GROUP_CH) ** -0.5)
    s5_c_re = nrm((DEPTH, S5_GROUPS, S5_GROUP_CH, S5_STATE), (2.0 * S5_STATE) ** -0.5)
    s5_c_im = nrm((DEPTH, S5_GROUPS, S5_GROUP_CH, S5_STATE), (2.0 * S5_STATE) ** -0.5)
    s5_d = nrm((DEPTH, S5_WIDTH), 1.0)
    s5_w_glu = nrm((DEPTH, S5_WIDTH, S5_WIDTH), S5_WIDTH ** -0.5)
    s5_b_glu = nrm((DEPTH, S5_WIDTH), 0.01)
    ssd_conv_w = nrm((DEPTH, SSD_CONV, SSD_CONV_DIM), SSD_CONV ** -0.5)
    ssd_conv_b = nrm((DEPTH, SSD_CONV_DIM), 0.01)
    dt0 = jnp.exp(unif((DEPTH, SSD_HEADS), float(np.log(1e-3)), float(np.log(1e-1))))
    ssd_dt_bias = dt0 + jnp.log(-jnp.expm1(-dt0))
    ssd_a_log = jnp.log(unif((DEPTH, SSD_HEADS), 1.0, 16.0))
    ssd_d = 1.0 + nrm((DEPTH, SSD_HEADS), 0.1)
    ssd_norm_g = 1.0 + nrm((DEPTH, SSD_WIDTH), 0.02)
    w_out = nrm((DEPTH, MIX_WIDTH, D_MODEL), MIX_WIDTH ** -0.5)
    norm2_g = 1.0 + nrm((DEPTH, D_MODEL), 0.02)
    w_gate = nrm((DEPTH, D_MODEL, D_FF), D_MODEL ** -0.5)
    w_up = nrm((DEPTH, D_MODEL, D_FF), D_MODEL ** -0.5)
    w_down = nrm((DEPTH, D_FF, D_MODEL), D_FF ** -0.5)
    final_norm_g = 1.0 + nrm((D_MODEL,), 0.02)
    return {'x_prompt': x_prompt, 'x_sample': x_sample,
            'state_s5_re': state_s5_re, 'state_s5_im': state_s5_im, 'state_ret': state_ret,
            'state_ssd': state_ssd, 'cache_ssd_conv': cache_ssd_conv,
            'norm1_g': norm1_g, 'w_in': w_in,
            's5_lam_re': s5_lam_re, 's5_lam_im': s5_lam_im, 's5_log_dt': s5_log_dt,
            's5_b_re': s5_b_re, 's5_b_im': s5_b_im, 's5_c_re': s5_c_re, 's5_c_im': s5_c_im,
            's5_d': s5_d, 's5_w_glu': s5_w_glu, 's5_b_glu': s5_b_glu,
            'ssd_conv_w': ssd_conv_w, 'ssd_conv_b': ssd_conv_b, 'ssd_dt_bias': ssd_dt_bias,
            'ssd_a_log': ssd_a_log, 'ssd_d': ssd_d, 'ssd_norm_g': ssd_norm_g,
            'w_out': w_out, 'norm2_g': norm2_g, 'w_gate': w_gate, 'w_up': w_up, 'w_down': w_down,
            'final_norm_g': final_norm_g}


def reference(x_prompt, x_sample, state_s5_re, state_s5_im, state_ret, state_ssd, cache_ssd_conv,
              norm1_g, w_in, s5_lam_re, s5_lam_im, s5_log_dt, s5_b_re, s5_b_im, s5_c_re, s5_c_im,
              s5_d, s5_w_glu, s5_b_glu, ssd_conv_w, ssd_conv_b, ssd_dt_bias, ssd_a_log, ssd_d, ssd_norm_g,
              w_out, norm2_g, w_gate, w_up, w_down, final_norm_g):
    p = {'norm1_g': norm1_g, 'w_in': w_in, 's5_lam_re': s5_lam_re, 's5_lam_im': s5_lam_im,
         's5_log_dt': s5_log_dt, 's5_b_re': s5_b_re, 's5_b_im': s5_b_im, 's5_c_re': s5_c_re,
         's5_c_im': s5_c_im, 's5_d': s5_d, 's5_w_glu': s5_w_glu, 's5_b_glu': s5_b_glu,
         'ssd_conv_w': ssd_conv_w, 'ssd_conv_b': ssd_conv_b, 'ssd_dt_bias': ssd_dt_bias,
         'ssd_a_log': ssd_a_log, 'ssd_d': ssd_d, 'ssd_norm_g': ssd_norm_g, 'w_out': w_out,
         'norm2_g': norm2_g, 'w_gate': w_gate, 'w_up': w_up, 'w_down': w_down}

    x = x_prompt
    p_states = []
    for l in range(DEPTH):
        x, st = trunk_layer(x, zero_state(x_prompt.shape[0]), p, l, 0)
        p_states.append(st)
    y_prompt = rmsnorm(x, final_norm_g)

    x = x_sample
    s_states = []
    for l in range(DEPTH):
        st_in = (state_s5_re[l], state_s5_im[l], state_ret[l], state_ssd[l], cache_ssd_conv[l])
        x, st = trunk_layer(x, st_in, p, l, PAST_LEN)
        s_states.append(st)
    y_sample = rmsnorm(x, final_norm_g)

    p_s5_re = jnp.stack([s[0] for s in p_states])
    p_s5_im = jnp.stack([s[1] for s in p_states])
    p_ret = jnp.stack([s[2] for s in p_states])
    p_ssd = jnp.stack([s[3] for s in p_states])
    p_conv = jnp.stack([s[4] for s in p_states])
    s_s5_re = jnp.stack([s[0] for s in s_states])
    s_s5_im = jnp.stack([s[1] for s in s_states])
    s_ret = jnp.stack([s[2] for s in s_states])
    s_ssd = jnp.stack([s[3] for s in s_states])
    s_conv = jnp.stack([s[4] for s in s_states])
    return (y_prompt, y_sample, p_s5_re, p_s5_im, p_ret, p_ssd, p_conv, s_s5_re, s_s5_im, s_ret, s_ssd, s_conv)
```

```python
import functools
import math

import jax
import jax.numpy as jnp
from jax import lax
from jax.experimental import pallas as pl
from jax.experimental.pallas import tpu as pltpu

F32 = jnp.float32
BF16 = jnp.bfloat16

D_MODEL = 1024
EPS = 1e-6
ROPE_BASE = 10000.0
S5_WIDTH = 256
S5_GROUPS = 16
S5_GROUP_CH = 16
S5_STATE = 64
S5_LANES = S5_GROUPS * S5_STATE
RET_HEADS = 4
RET_DK = 64
RET_DV = 64
RET_WIDTH = RET_HEADS * RET_DV
SSD_WIDTH = 512
SSD_HEADDIM = 64
SSD_HEADS = 8
SSD_GROUPS = 2
SSD_DSTATE = 128
SSD_CONV = 4
SSD_CONV_DIM = SSD_WIDTH + 2 * SSD_GROUPS * SSD_DSTATE
D_FF = 2816
LANES = 128
SUBLANES = 8

U0 = 0
Q0 = U0 + S5_WIDTH
K0 = Q0 + RET_HEADS * RET_DK
V0 = K0 + RET_HEADS * RET_DK
G0 = V0 + RET_WIDTH
Z0 = G0 + RET_WIDTH
X0 = Z0 + SSD_WIDTH
DT0 = X0 + SSD_CONV_DIM
IN_PAD = DT0 + LANES

VMEM_LIMIT = 60000 * 1024


def _dot(a, b):
    return jnp.dot(a, b, preferred_element_type=F32)


def _dot_nt(a, b):
    return lax.dot_general(a, b, (((1,), (1,)), ((), ())), preferred_element_type=F32)


def _sigmoid(x):
    return 1.0 / (1.0 + jnp.exp(-x))


def _silu(x):
    return x * _sigmoid(x)


def _gelu_tanh(x):
    c = math.sqrt(2.0 / math.pi)
    return x * (0.5 * (1.0 + jnp.tanh(c * (x + 0.044715 * (x * x * x)))))


def _softplus(x):
    return jnp.maximum(x, 0.0) + jnp.log1p(jnp.exp(-jnp.abs(x)))


def _rms(x, g):
    ms = jnp.mean(x * x, axis=-1, keepdims=True)
    return x * lax.rsqrt(ms + EPS) * g


def _split3(x):
    hi = x.astype(BF16)
    r1 = x - hi.astype(F32)
    mid = r1.astype(BF16)
    lo = (r1 - mid.astype(F32)).astype(BF16)
    return jnp.concatenate([hi, mid, lo], axis=-1)


def _dot_hilo(x, m):
    hi = x.astype(BF16)
    lo = (x - hi.astype(F32)).astype(BF16)
    return _dot(hi, m) + _dot(lo, m)


def _s5_param_kernel(lr_ref, li_ref, ldt_ref, br_ref, bi_ref, ptr_ref, pti_ref, bbr_ref, bbi_ref):
    depth = lr_ref.shape[0]
    for l in range(depth):
        lr = lr_ref[l:l + 1, :]
        li = li_ref[l:l + 1, :]
        dt = jnp.exp(ldt_ref[l:l + 1, :])
        mag = jnp.exp(lr * dt)
        ar = mag * jnp.cos(li * dt)
        ai = mag * jnp.sin(li * dt)
        den = lr * lr + li * li
        nr = ar - 1.0
        ni = ai
        kr = (nr * lr + ni * li) / den
        ki = (ni * lr - nr * li) / den
        br = br_ref[l]
        bi = bi_ref[l]
        bbr_ref[l] = kr * br - ki * bi
        bbi_ref[l] = kr * bi + ki * br
        pr, pi = ar, ai
        rows_r, rows_i = [pr], [pi]
        for _ in range(SUBLANES - 1):
            pr, pi = pr * ar - pi * ai, pr * ai + pi * ar
            rows_r.append(pr)
            rows_i.append(pi)
        ptr_ref[l] = jnp.concatenate(rows_r, axis=0)
        pti_ref[l] = jnp.concatenate(rows_i, axis=0)


def _s5_params(lam_re, lam_im, log_dt, b_re, b_im):
    depth = lam_re.shape[0]
    lr = lam_re.reshape(depth, S5_LANES)
    li = lam_im.reshape(depth, S5_LANES)
    ldt = jnp.repeat(log_dt, S5_STATE, axis=-1)
    brt = jnp.transpose(b_re, (0, 3, 1, 2)).reshape(depth, S5_GROUP_CH, S5_LANES)
    bit = jnp.transpose(b_im, (0, 3, 1, 2)).reshape(depth, S5_GROUP_CH, S5_LANES)
    out_shape = (jax.ShapeDtypeStruct((depth, SUBLANES, S5_LANES), F32),
                 jax.ShapeDtypeStruct((depth, SUBLANES, S5_LANES), F32),
                 jax.ShapeDtypeStruct((depth, S5_GROUP_CH, S5_LANES), F32),
                 jax.ShapeDtypeStruct((depth, S5_GROUP_CH, S5_LANES), F32))
    return pl.pallas_call(_s5_param_kernel, out_shape=out_shape, name="s5_params")(lr, li, ldt, brt, bit)


def _mixer_kernel(x_ref, g1_ref, win_ref, cos_ref, sin_ref,
                  ptr_ref, pti_ref, bb_ref, ctr_ref, cti_ref, s5d_ref, wglu_ref, bglu_ref,
                  dec_ref, qw_ref, kw_ref, cdc_ref, mbd_ref, mavg_ref, hmq_ref, hmv_ref,
                  tri_ref, e3_ref,
                  cw_ref, cb_ref, dtb_ref, alog_ref, dskip_ref, ng_ref, wout_ref,
                  s5r_in, s5i_in, ret_in, ssd_in, conv_in,
                  xo_ref, s5r_out, s5i_out, ret_out, ssd_out, conv_out,
                  proj_ref, mix_ref, hb_ref, xp_ref, *, nb, tile, chunk):
    C = chunk
    nchunk = tile // C
    ic = pl.program_id(1)

    @pl.when(ic == 0)
    def _init_states():
        s5r_out[...] = s5r_in[...]
        s5i_out[...] = s5i_in[...]
        ret_out[...] = ret_in[...]
        ssd_out[...] = ssd_in[...]
        conv_out[...] = conv_in[...]

    x = x_ref[...]
    hn = _rms(x, g1_ref[...])
    proj_ref[...] = _dot(hn.astype(BF16), win_ref[...])

    row8 = lax.broadcasted_iota(jnp.int32, (SUBLANES, S5_LANES), 0)
    causal = (lax.broadcasted_iota(jnp.int32, (C, C), 0) >= lax.broadcasted_iota(jnp.int32, (C, C), 1))
    lane128 = lax.broadcasted_iota(jnp.int32, (1, LANES), 1)

    def segment(bi, ci, r0):
        rows = pl.ds(r0, C)

        u = proj_ref[rows, U0:U0 + S5_WIDTH]
        hb_ref[...] = _dot(u.astype(BF16), bb_ref[...])
        pr = ptr_ref[...]
        pi = pti_ref[...]
        steps = ((1, pr[0:1], pi[0:1]), (2, pr[1:2], pi[1:2]), (4, pr[3:4], pi[3:4]))

        def blk(i, carry):
            hr, hi = carry
            o = pl.multiple_of(i * SUBLANES, SUBLANES)
            xr = hb_ref[pl.ds(o, SUBLANES), 0:S5_LANES]
            xi = hb_ref[pl.ds(o, SUBLANES), S5_LANES:2 * S5_LANES]
            for s, ar, ai in steps:
                sr = jnp.where(row8 >= s, pltpu.roll(xr, s, 0), 0.0)
                si = jnp.where(row8 >= s, pltpu.roll(xi, s, 0), 0.0)
                xr, xi = xr + ar * sr - ai * si, xi + ar * si + ai * sr
            nr = xr + pr * hr - pi * hi
            ni = xi + pr * hi + pi * hr
            hb_ref[pl.ds(o, SUBLANES), 0:S5_LANES] = nr
            hb_ref[pl.ds(o, SUBLANES), S5_LANES:2 * S5_LANES] = ni
            return nr[SUBLANES - 1:SUBLANES], ni[SUBLANES - 1:SUBLANES]

        hr, hi = lax.fori_loop(0, C // SUBLANES, blk, (s5r_out[bi], s5i_out[bi]))
        s5r_out[bi] = hr
        s5i_out[bi] = hi
        sr_all = hb_ref[:, 0:S5_LANES].astype(BF16)
        si_all = hb_ref[:, S5_LANES:2 * S5_LANES].astype(BF16)
        y = _dot_nt(sr_all, ctr_ref[...]) - _dot_nt(si_all, cti_ref[...])
        y = y + s5d_ref[...] * u
        zg = _gelu_tanh(y)
        gl = _dot(zg.astype(BF16), wglu_ref[...]) + bglu_ref[...]
        mix_ref[rows, 0:S5_WIDTH] = zg * _sigmoid(gl)

        half = RET_HEADS * RET_DK // 2
        q1 = proj_ref[rows, Q0:Q0 + half]
        q2 = proj_ref[rows, Q0 + half:Q0 + 2 * half]
        k1 = proj_ref[rows, K0:K0 + half]
        k2 = proj_ref[rows, K0 + half:K0 + 2 * half]
        cs_ = cos_ref[pl.ds(ci * C, C), :]
        sn_ = sin_ref[pl.ds(ci * C, C), :]
        qr = jnp.concatenate([q1 * cs_ - q2 * sn_, q1 * sn_ + q2 * cs_], axis=-1)
        kr = jnp.concatenate([k1 * cs_ - k2 * sn_, k1 * sn_ + k2 * cs_], axis=-1) * (RET_DK ** -0.5)
        v = proj_ref[rows, V0:V0 + RET_WIDTH]
        gate = proj_ref[rows, G0:G0 + RET_WIDTH]
        s_prev = ret_out[bi]
        kb = kr.astype(BF16)
        o = _dot((qr * qw_ref[...]).astype(BF16), s_prev.astype(BF16))
        for h in range(RET_HEADS):
            sc = _dot_nt((qr * hmq_ref[h]).astype(BF16), kb) * dec_ref[h]
            o = o + _dot(sc.astype(BF16), (v * hmv_ref[h]).astype(BF16))
        kt = (kr * kw_ref[...]).T
        kv = _dot(kt.astype(BF16), v.astype(BF16))
        ret_out[bi] = s_prev * cdc_ref[...] + kv * mbd_ref[...]
        mavg = mavg_ref[...]
        mu = _dot_hilo(o, mavg)
        dlt = o - mu
        var = _dot_hilo(dlt * dlt, mavg)
        mix_ref[rows, S5_WIDTH:S5_WIDTH + RET_WIDTH] = _silu(gate) * (dlt * lax.rsqrt(var + EPS))

        z = proj_ref[rows, Z0:Z0 + SSD_WIDTH]
        xbc = proj_ref[rows, X0:X0 + SSD_CONV_DIM]
        xp_ref[0:SUBLANES, :] = conv_out[bi]
        xp_ref[SUBLANES:SUBLANES + C, :] = xbc
        cw = cw_ref[...]
        acc = cb_ref[...] + cw[3:4] * xbc
        for i in range(SSD_CONV - 1):
            lo = SUBLANES - (SSD_CONV - 1) + i
            acc = acc + cw[i:i + 1] * xp_ref[lo:lo + C, :]
        conv_out[bi] = xp_ref[C:C + SUBLANES, :]
        xc = _silu(acc)
        xs = xc[:, 0:SSD_WIDTH]
        ngl = SSD_DSTATE
        bm = xc[:, SSD_WIDTH:SSD_WIDTH + SSD_GROUPS * ngl]
        cm = xc[:, SSD_WIDTH + SSD_GROUPS * ngl:SSD_CONV_DIM]
        dt = _softplus(proj_ref[rows, DT0:DT0 + LANES] + dtb_ref[...])
        a_row = jnp.where(lane128 < SSD_HEADS, -jnp.exp(alog_ref[...]), 0.0)
        dta = dt * a_row
        cs3 = _dot(tri_ref[...], _split3(dta))
        cs = cs3[:, 0:LANES] + cs3[:, LANES:2 * LANES] + cs3[:, 2 * LANES:3 * LANES]
        dt_e = _dot(_split3(dt), e3_ref[...])
        cs_e = _dot(_split3(cs), e3_ref[...])
        last_e = cs_e[C - 1:C, :]
        ecs_e = jnp.exp(cs_e)
        wend_e = jnp.exp(last_e - cs_e)
        cdec_e = jnp.exp(last_e)
        xdt = xs * dt_e
        cst = cs.T
        hpg = SSD_HEADS // SSD_GROUPS
        gw = hpg * SSD_HEADDIM
        ys = []
        for g in range(SSD_GROUPS):
            cm_g = cm[:, g * ngl:(g + 1) * ngl].astype(BF16)
            bm_g = bm[:, g * ngl:(g + 1) * ngl]
            cbm = _dot_nt(cm_g, bm_g.astype(BF16))
            st = ssd_out[bi, g]
            xdt_g = xdt[:, g * gw:(g + 1) * gw]
            yg = _dot(cm_g, st.astype(BF16)) * ecs_e[:, g * gw:(g + 1) * gw]
            for hl in range(hpg):
                h = g * hpg + hl
                seg = cs[:, h:h + 1] - cst[h:h + 1, :]
                lm = jnp.where(causal, jnp.exp(jnp.minimum(seg, 0.0)), 0.0)
                yg = yg + _dot((cbm * lm).astype(BF16), (xdt_g * hmv_ref[hl]).astype(BF16))
            ys.append(yg)
            new = _dot(bm_g.T.astype(BF16), (xdt_g * wend_e[:, g * gw:(g + 1) * gw]).astype(BF16))
            ssd_out[bi, g] = st * cdec_e[:, g * gw:(g + 1) * gw] + new
        yss = jnp.concatenate(ys, axis=-1) + dskip_ref[...] * xs
        mix_ref[rows, S5_WIDTH + RET_WIDTH:D_MODEL] = _rms(yss * _silu(z), ng_ref[...])

    if nb == 1:
        for ci in range(nchunk):
            segment(0, ci, ci * C)
    else:
        def seg_body(j, carry):
            segment(j, 0, pl.multiple_of(j * C, C))
            return carry
        lax.fori_loop(0, nb, seg_body, 0)

    xo_ref[...] = x + _dot(mix_ref[...].astype(BF16), wout_ref[...])


def _const_spec(shape, l=None):
    if l is None:
        return pl.BlockSpec(shape, lambda b, c: (0,) * len(shape), pipeline_mode=pl.Buffered(1))
    nd = len(shape)
    return pl.BlockSpec((None,) + tuple(shape), lambda b, c: (l,) + (0,) * nd, pipeline_mode=pl.Buffered(1))


def _mixer_call(x2d, states, prm, tabs, l, *, batch, seq, nb, tile, chunk):
    C = chunk
    n_b = batch // nb
    n_t = seq // tile
    rows = nb * tile
    row_spec = pl.BlockSpec((rows, D_MODEL), lambda b, c: (b * n_t + c, 0))
    st_shapes = [(1, S5_LANES), (1, S5_LANES), (RET_HEADS * RET_DK, RET_WIDTH),
                 (SSD_GROUPS, SSD_DSTATE, SSD_WIDTH // SSD_GROUPS), (SUBLANES, SSD_CONV_DIM)]
    st_specs = [pl.BlockSpec((nb,) + s, lambda b, c, _n=len(s): (b,) + (0,) * _n) for s in st_shapes]
    in_specs = [
        row_spec,
        _const_spec((1, D_MODEL), l),
        _const_spec((D_MODEL, IN_PAD), l),
        pl.BlockSpec((tile, LANES), lambda b, c: (c, 0)),
        pl.BlockSpec((tile, LANES), lambda b, c: (c, 0)),
        _const_spec((SUBLANES, S5_LANES), l), _const_spec((SUBLANES, S5_LANES), l),
        _const_spec((S5_WIDTH, 2 * S5_LANES), l),
        _const_spec((S5_WIDTH, S5_LANES), l), _const_spec((S5_WIDTH, S5_LANES), l),
        _const_spec((1, S5_WIDTH), l), _const_spec((S5_WIDTH, S5_WIDTH), l), _const_spec((1, S5_WIDTH), l),
        _const_spec((RET_HEADS, C, C)), _const_spec((C, RET_WIDTH)), _const_spec((C, RET_WIDTH)),
        _const_spec((1, RET_WIDTH)), _const_spec((RET_WIDTH, RET_WIDTH)), _const_spec((RET_WIDTH, RET_WIDTH)),
        _const_spec((RET_HEADS, 1, RET_WIDTH)), _const_spec((RET_HEADS, 1, RET_WIDTH)),
        _const_spec((C, C)), _const_spec((3 * LANES, SSD_WIDTH)),
        _const_spec((SSD_CONV, SSD_CONV_DIM), l), _const_spec((1, SSD_CONV_DIM), l),
        _const_spec((1, LANES), l), _const_spec((1, LANES), l),
        _const_spec((1, SSD_WIDTH), l), _const_spec((1, SSD_WIDTH), l),
        _const_spec((D_MODEL, D_MODEL), l),
    ] + st_specs
    out_specs = [row_spec] + st_specs
    out_shape = [jax.ShapeDtypeStruct(x2d.shape, F32)] + [jax.ShapeDtypeStruct(s.shape, F32) for s in states]
    scratch = [pltpu.VMEM((rows, IN_PAD), F32), pltpu.VMEM((rows, D_MODEL), F32),
               pltpu.VMEM((C, 2 * S5_LANES), F32), pltpu.VMEM((C + SUBLANES, SSD_CONV_DIM), F32)]
    kern = functools.partial(_mixer_kernel, nb=nb, tile=tile, chunk=C)
    outs = pl.pallas_call(
        kern, grid=(n_b, n_t), in_specs=in_specs, out_specs=out_specs, out_shape=out_shape,
        scratch_shapes=scratch, name=f"mixer_c{C}",
        compiler_params=pltpu.CompilerParams(dimension_semantics=("arbitrary", "arbitrary"),
                                             vmem_limit_bytes=VMEM_LIMIT),
    )(x2d, prm["g1"], prm["win"], tabs["cos"], tabs["sin"],
      prm["ptr"], prm["pti"], prm["bb"], prm["ctr"], prm["cti"], prm["s5d"], prm["wglu"], prm["bglu"],
      tabs["dec"], tabs["qw"], tabs["kw"], tabs["cdc"], tabs["mbd"], tabs["mavg"], tabs["hmq"], tabs["hmv"],
      tabs["tri"], tabs["e3"],
      prm["cw"], prm["cb"], prm["dtb"], prm["alog"], prm["dskip"], prm["ng"], prm["wout"],
      *states)
    return outs[0], tuple(outs[1:])


def _ffn_kernel(x_ref, g2_ref, wg_ref, wu_ref, wd_ref, gf_ref, o_ref, *, final):
    x = x_ref[...]
    hn = _rms(x, g2_ref[...]).astype(BF16)
    acc = x
    nhalf = 2
    w = D_FF // nhalf
    for i in range(nhalf):
        gt = _dot(hn, wg_ref[:, i * w:(i + 1) * w])
        up = _dot(hn, wu_ref[:, i * w:(i + 1) * w])
        acc = acc + _dot((_silu(gt) * up).astype(BF16), wd_ref[i * w:(i + 1) * w, :])
    if final:
        acc = _rms(acc, gf_ref[...])
    o_ref[...] = acc


def _ffn_call(x2d, prm, l, *, rows, final):
    n = x2d.shape[0] // rows
    row_spec = pl.BlockSpec((rows, D_MODEL), lambda i: (i, 0))

    def wspec(shape, layer=True):
        nd = len(shape)
        if layer:
            return pl.BlockSpec((None,) + shape, lambda i: (l,) + (0,) * nd, pipeline_mode=pl.Buffered(1))
        return pl.BlockSpec(shape, lambda i: (0,) * nd, pipeline_mode=pl.Buffered(1))

    return pl.pallas_call(
        functools.partial(_ffn_kernel, final=final), grid=(n,),
        in_specs=[row_spec, wspec((1, D_MODEL)), wspec((D_MODEL, D_FF)), wspec((D_MODEL, D_FF)),
                  wspec((D_FF, D_MODEL)), wspec((1, D_MODEL), layer=False)],
        out_specs=row_spec, out_shape=jax.ShapeDtypeStruct(x2d.shape, F32),
        name="ffn_final" if final else "ffn",
        compiler_params=pltpu.CompilerParams(dimension_semantics=("arbitrary",), vmem_limit_bytes=VMEM_LIMIT),
    )(x2d, prm["g2"], prm["wg"], prm["wu"], prm["wd"], prm["gf"])


def _in_perm():
    idx = list(range(0, S5_WIDTH))
    for base in (S5_WIDTH, S5_WIDTH + RET_HEADS * RET_DK):
        for hf in range(2):
            for h in range(RET_HEADS):
                idx += [base + h * RET_DK + hf * (RET_DK // 2) + d for d in range(RET_DK // 2)]
    start = S5_WIDTH + 2 * RET_HEADS * RET_DK
    idx += list(range(start, start + 2 * RET_WIDTH + SSD_WIDTH + SSD_CONV_DIM + SSD_HEADS))
    return jnp.asarray(idx, jnp.int32)


def _prepare(norm1_g, w_in, s5_lam_re, s5_lam_im, s5_log_dt, s5_b_re, s5_b_im, s5_c_re, s5_c_im,
             s5_d, s5_w_glu, s5_b_glu, ssd_conv_w, ssd_conv_b, ssd_dt_bias, ssd_a_log, ssd_d, ssd_norm_g,
             w_out, norm2_g, w_gate, w_up, w_down, final_norm_g):
    depth = w_in.shape[0]
    win = jnp.take(w_in, _in_perm(), axis=2)
    win = jnp.pad(win, ((0, 0), (0, 0), (0, IN_PAD - win.shape[2]))).astype(BF16)
    ptr, pti, bbr, bbi = _s5_params(s5_lam_re, s5_lam_im, s5_log_dt, s5_b_re, s5_b_im)
    rg = jnp.arange(S5_WIDTH)[:, None] // S5_GROUP_CH
    cg = jnp.arange(S5_LANES)[None, :] // S5_STATE
    blk = (rg == cg)[None]

    def bdiag_b(t):
        return jnp.where(blk, jnp.tile(t, (1, S5_GROUPS, 1)), 0.0)

    def bdiag_c(c):
        c2 = c.reshape(depth, S5_WIDTH, S5_STATE)
        return jnp.where(blk, jnp.tile(c2, (1, 1, S5_GROUPS)), 0.0)

    pad_l = ((0, 0), (0, 0), (0, LANES - SSD_HEADS))
    return {
        "g1": norm1_g[:, None, :], "win": win, "ptr": ptr, "pti": pti,
        "bb": jnp.concatenate([bdiag_b(bbr), bdiag_b(bbi)], axis=-1).astype(BF16),
        "ctr": bdiag_c(s5_c_re).astype(BF16), "cti": bdiag_c(s5_c_im).astype(BF16),
        "s5d": s5_d[:, None, :], "wglu": s5_w_glu.astype(BF16), "bglu": s5_b_glu[:, None, :],
        "cw": ssd_conv_w, "cb": ssd_conv_b[:, None, :],
        "dtb": jnp.pad(ssd_dt_bias[:, None, :], pad_l), "alog": jnp.pad(ssd_a_log[:, None, :], pad_l),
        "dskip": jnp.repeat(ssd_d, SSD_HEADDIM, axis=-1)[:, None, :], "ng": ssd_norm_g[:, None, :],
        "wout": w_out.astype(BF16), "g2": norm2_g[:, None, :],
        "wg": w_gate.astype(BF16), "wu": w_up.astype(BF16), "wd": w_down.astype(BF16),
        "gf": final_norm_g[None, :],
    }


def _tables(seq, chunk, t0):
    C = chunk
    lg = [math.log(1.0 - 2.0 ** (-5.0 - h)) for h in range(RET_HEADS)]
    lgv = jnp.asarray(lg, F32)
    i = jnp.arange(C, dtype=F32)
    rel = i[:, None] - i[None, :]
    dec = jnp.where(rel >= 0, jnp.exp(lgv[:, None, None] * jnp.maximum(rel, 0.0)), 0.0)
    lane = jnp.arange(RET_WIDTH)
    hq = (lane % (RET_WIDTH // 2)) // (RET_DK // 2)
    hv = lane // RET_DV
    qw = jnp.exp(lgv[hq][None, :] * (i + 1.0)[:, None])
    kw = jnp.exp(lgv[hq][None, :] * (C - 1.0 - i)[:, None])
    cdc = jnp.exp(lgv[hv] * C)[None, :]
    mbd = (hq[:, None] == hv[None, :]).astype(F32)
    mavg = jnp.where(hv[:, None] == hv[None, :], 1.0 / RET_DV, 0.0).astype(BF16)
    hmq = (hq[None, :] == jnp.arange(RET_HEADS)[:, None]).astype(F32)[:, None, :]
    hmv = (hv[None, :] == jnp.arange(RET_HEADS)[:, None]).astype(F32)[:, None, :]
    tri = (i[:, None] >= i[None, :]).astype(BF16)
    r = jnp.arange(3 * LANES) % LANES
    e3 = (r[:, None] == (jnp.arange(SSD_WIDTH) // SSD_HEADDIM)[None, :]).astype(BF16)
    half = RET_DK // 2
    inv_freq = ROPE_BASE ** (-jnp.arange(half, dtype=F32) / half)
    pos = (t0 + jnp.arange(seq)).astype(F32)
    ang = pos[:, None] * jnp.tile(inv_freq, RET_HEADS)[None, :]
    return {"dec": dec, "qw": qw, "kw": kw, "cdc": cdc, "mbd": mbd, "mavg": mavg, "hmq": hmq, "hmv": hmv,
            "tri": tri, "e3": e3, "cos": jnp.cos(ang), "sin": jnp.sin(ang)}


def _ret_to_kernel(s):
    b = s.shape[0]
    half = RET_DK // 2
    t = s.reshape(b, RET_HEADS, 2, half, RET_DV).transpose(0, 2, 1, 3, 4)
    eye = jnp.eye(RET_HEADS, dtype=bool)[None, None, :, None, :, None]
    full = jnp.where(eye, t[:, :, :, :, None, :], 0.0)
    return full.reshape(b, RET_HEADS * RET_DK, RET_WIDTH)


def _ret_from_kernel(s):
    b = s.shape[0]
    half = RET_DK // 2
    t = s.reshape(b, 2, RET_HEADS, half, RET_HEADS, RET_DV)
    d = jnp.diagonal(t, axis1=2, axis2=4)
    return d.transpose(0, 4, 1, 2, 3).reshape(b, RET_HEADS, RET_DK, RET_DV)


def _ssd_to_kernel(s):
    b = s.shape[0]
    hpg = SSD_HEADS // SSD_GROUPS
    t = s.reshape(b, SSD_GROUPS, hpg, SSD_HEADDIM, SSD_DSTATE).transpose(0, 1, 4, 2, 3)
    return t.reshape(b, SSD_GROUPS, SSD_DSTATE, hpg * SSD_HEADDIM)


def _ssd_from_kernel(s):
    b = s.shape[0]
    hpg = SSD_HEADS // SSD_GROUPS
    t = s.reshape(b, SSD_GROUPS, SSD_DSTATE, hpg, SSD_HEADDIM).transpose(0, 1, 3, 4, 2)
    return t.reshape(b, SSD_HEADS, SSD_HEADDIM, SSD_DSTATE)


def _plan(batch, seq):
    if seq >= 512:
        return 1, 512, 256
    nb = max(1, min(batch, 256 // seq))
    while batch % nb:
        nb -= 1
    return nb, seq, seq


def _trunk(x, states_in, prm, t0):
    batch, seq, _ = x.shape
    nb, tile, chunk = _plan(batch, seq)
    tabs = _tables(seq, chunk, t0)
    depth = prm["win"].shape[0]
    x2d = x.reshape(batch * seq, D_MODEL)
    ffn_rows = min(512, batch * seq)
    outs = []
    for l in range(depth):
        st = states_in[l]
        x2d, st_out = _mixer_call(x2d, st, prm, tabs, l, batch=batch, seq=seq, nb=nb, tile=tile, chunk=chunk)
        x2d = _ffn_call(x2d, prm, l, rows=ffn_rows, final=(l == depth - 1))
        outs.append(st_out)
    return x2d.reshape(batch, seq, D_MODEL), outs


def _zero_states(batch):
    return (jnp.zeros((batch, 1, S5_LANES), F32), jnp.zeros((batch, 1, S5_LANES), F32),
            jnp.zeros((batch, RET_HEADS * RET_DK, RET_WIDTH), F32),
            jnp.zeros((batch, SSD_GROUPS, SSD_DSTATE, SSD_WIDTH // SSD_GROUPS), F32),
            jnp.zeros((batch, SUBLANES, SSD_CONV_DIM), F32))


def _states_out(outs):
    s5r = jnp.stack([o[0].reshape(-1, S5_GROUPS, S5_STATE) for o in outs])
    s5i = jnp.stack([o[1].reshape(-1, S5_GROUPS, S5_STATE) for o in outs])
    ret = jnp.stack([_ret_from_kernel(o[2]) for o in outs])
    ssd = jnp.stack([_ssd_from_kernel(o[3]) for o in outs])
    conv = jnp.stack([o[4][:, SUBLANES - (SSD_CONV - 1):, :] for o in outs])
    return s5r, s5i, ret, ssd, conv


def kernel(x_prompt, x_sample, state_s5_re, state_s5_im, state_ret, state_ssd, cache_ssd_conv, norm1_g, w_in, s5_lam_re, s5_lam_im, s5_log_dt, s5_b_re, s5_b_im, s5_c_re, s5_c_im, s5_d, s5_w_glu, s5_b_glu, ssd_conv_w, ssd_conv_b, ssd_dt_bias, ssd_a_log, ssd_d, ssd_norm_g, w_out, norm2_g, w_gate, w_up, w_down, final_norm_g):
    prm = _prepare(norm1_g, w_in, s5_lam_re, s5_lam_im, s5_log_dt, s5_b_re, s5_b_im, s5_c_re, s5_c_im,
                   s5_d, s5_w_glu, s5_b_glu, ssd_conv_w, ssd_conv_b, ssd_dt_bias, ssd_a_log, ssd_d, ssd_norm_g,
                   w_out, norm2_g, w_gate, w_up, w_down, final_norm_g)
    depth = w_in.shape[0]
    past_len = 1024

    bp = x_prompt.shape[0]
    y_prompt, p_outs = _trunk(x_prompt, [_zero_states(bp)] * depth, prm, 0)

    bs = x_sample.shape[0]
    pad_conv = ((0, 0), (SUBLANES - (SSD_CONV - 1), 0), (0, 0))
    s_in = [(state_s5_re[l].reshape(bs, 1, S5_LANES), state_s5_im[l].reshape(bs, 1, S5_LANES),
             _ret_to_kernel(state_ret[l]), _ssd_to_kernel(state_ssd[l]), jnp.pad(cache_ssd_conv[l], pad_conv))
            for l in range(depth)]
    y_sample, s_outs = _trunk(x_sample, s_in, prm, past_len)

    return (y_prompt, y_sample) + _states_out(p_outs) + _states_out(s_outs)
```

```python
import functools
import math

import jax
import jax.numpy as jnp
from jax import lax
from jax.experimental import pallas as pl
from jax.experimental.pallas import tpu as pltpu

F32 = jnp.float32
BF16 = jnp.bfloat16

D_MODEL = 1024
EPS = 1e-6
ROPE_BASE = 10000.0
S5_WIDTH = 256
S5_GROUPS = 16
S5_GROUP_CH = 16
S5_STATE = 64
S5_LANES = S5_GROUPS * S5_STATE
RET_HEADS = 4
RET_DK = 64
RET_DV = 64
RET_WIDTH = RET_HEADS * RET_DV
SSD_WIDTH = 512
SSD_HEADDIM = 64
SSD_HEADS = 8
SSD_GROUPS = 2
SSD_DSTATE = 128
SSD_CONV = 4
SSD_CONV_DIM = SSD_WIDTH + 2 * SSD_GROUPS * SSD_DSTATE
D_FF = 2816
LANES = 128
SUBLANES = 8

U0 = 0
Q0 = U0 + S5_WIDTH
K0 = Q0 + RET_HEADS * RET_DK
V0 = K0 + RET_HEADS * RET_DK
G0 = V0 + RET_WIDTH
Z0 = G0 + RET_WIDTH
X0 = Z0 + SSD_WIDTH
DT0 = X0 + SSD_CONV_DIM
IN_PAD = DT0 + LANES

VMEM_LIMIT = 60000 * 1024


def _dot(a, b):
    return jnp.dot(a, b, preferred_element_type=F32)


def _dot_nt(a, b):
    return lax.dot_general(a, b, (((1,), (1,)), ((), ())), preferred_element_type=F32)


def _sigmoid(x):
    return 1.0 / (1.0 + jnp.exp(-x))


def _silu(x):
    return x * _sigmoid(x)


def _gelu_tanh(x):
    c = math.sqrt(2.0 / math.pi)
    return x * (0.5 * (1.0 + jnp.tanh(c * (x + 0.044715 * (x * x * x)))))


def _softplus(x):
    return jnp.maximum(x, 0.0) + jnp.log1p(jnp.exp(-jnp.abs(x)))


def _rms(x, g):
    ms = jnp.mean(x * x, axis=-1, keepdims=True)
    return x * lax.rsqrt(ms + EPS) * g


def _split3(x):
    hi = x.astype(BF16)
    r1 = x - hi.astype(F32)
    mid = r1.astype(BF16)
    lo = (r1 - mid.astype(F32)).astype(BF16)
    return jnp.concatenate([hi, mid, lo], axis=-1)


def _s5_param_kernel(lr_ref, li_ref, ldt_ref, br_ref, bi_ref, ar_ref, ai_ref, bbr_ref, bbi_ref):
    depth = lr_ref.shape[0]
    for l in range(depth):
        lr = lr_ref[l:l + 1, :]
        li = li_ref[l:l + 1, :]
        dt = jnp.exp(ldt_ref[l:l + 1, :])
        mag = jnp.exp(lr * dt)
        ar = mag * jnp.cos(li * dt)
        ai = mag * jnp.sin(li * dt)
        den = lr * lr + li * li
        nr = ar - 1.0
        ni = ai
        kr = (nr * lr + ni * li) / den
        ki = (ni * lr - nr * li) / den
        br = br_ref[l]
        bi = bi_ref[l]
        bbr_ref[l] = kr * br - ki * bi
        bbi_ref[l] = kr * bi + ki * br
        ar_ref[l:l + 1, :] = ar
        ai_ref[l:l + 1, :] = ai


def _s5_params(lam_re, lam_im, log_dt, b_re, b_im):
    depth = lam_re.shape[0]
    lr = lam_re.reshape(depth, S5_LANES)
    li = lam_im.reshape(depth, S5_LANES)
    ldt = jnp.repeat(log_dt, S5_STATE, axis=-1)
    brt = jnp.transpose(b_re, (0, 3, 1, 2)).reshape(depth, S5_GROUP_CH, S5_LANES)
    bit = jnp.transpose(b_im, (0, 3, 1, 2)).reshape(depth, S5_GROUP_CH, S5_LANES)
    out_shape = (jax.ShapeDtypeStruct((depth, S5_LANES), F32),
                 jax.ShapeDtypeStruct((depth, S5_LANES), F32),
                 jax.ShapeDtypeStruct((depth, S5_GROUP_CH, S5_LANES), F32),
                 jax.ShapeDtypeStruct((depth, S5_GROUP_CH, S5_LANES), F32))
    return pl.pallas_call(_s5_param_kernel, out_shape=out_shape, name="s5_params")(lr, li, ldt, brt, bit)


def _mixer_kernel(x_ref, g1_ref, win_ref, cos_ref, sin_ref,
                  a8r_ref, a8i_ref, bb_ref, ctr_ref, cti_ref, s5d_ref, wglu_ref, bglu_ref,
                  dec_ref, qw_ref, kw_ref, cdc_ref, mbd_ref, mavg_ref, hmq_ref, hmv_ref,
                  tri_ref, e3_ref,
                  cw_ref, cb_ref, dtb_ref, alog_ref, dskip_ref, ng_ref, wout_ref,
                  s5r_in, s5i_in, ret_in, ssd_in, conv_in,
                  xo_ref, s5r_out, s5i_out, ret_out, ssd_out, conv_out,
                  proj_ref, mix_ref, hb_ref, xp_ref, *, nb, tile, chunk):
    C = chunk
    nchunk = tile // C
    ic = pl.program_id(1)

    @pl.when(ic == 0)
    def _init_states():
        s5r_out[...] = s5r_in[...]
        s5i_out[...] = s5i_in[...]
        ret_out[...] = ret_in[...]
        ssd_out[...] = ssd_in[...]
        conv_out[...] = conv_in[...]

    x = x_ref[...]
    hn = _rms(x, g1_ref[...])
    proj_ref[...] = _dot(hn.astype(BF16), win_ref[...])

    causal = (lax.broadcasted_iota(jnp.int32, (C, C), 0) >= lax.broadcasted_iota(jnp.int32, (C, C), 1))
    lane128 = lax.broadcasted_iota(jnp.int32, (1, LANES), 1)

    def segment(bi, ci, r0):
        rows = pl.ds(r0, C)

        u = proj_ref[rows, U0:U0 + S5_WIDTH]
        bu = _dot(u.astype(BF16), bb_ref[...])
        nslab = S5_LANES // LANES
        pitch = C + SUBLANES
        for s in range(2 * nslab):
            hb_ref[s * pitch:s * pitch + C, :] = bu[:, s * LANES:(s + 1) * LANES]
        ar = a8r_ref[...]
        ai = a8i_ref[...]
        hr = s5r_out[bi]
        hi = s5i_out[bi]
        for t in range(C):
            ld_r = pl.ds(t, nslab, stride=pitch)
            ld_i = pl.ds(nslab * pitch + t, nslab, stride=pitch)
            hr, hi = (ar * hr - ai * hi + hb_ref[ld_r, :], ar * hi + ai * hr + hb_ref[ld_i, :])
            hb_ref[ld_r, :] = hr
            hb_ref[ld_i, :] = hi
        s5r_out[bi] = hr
        s5i_out[bi] = hi
        y = jnp.zeros((C, S5_WIDTH), F32)
        for k in range(nslab // 2):
            lo = 2 * k * pitch
            hre = jnp.concatenate([hb_ref[lo:lo + C, :], hb_ref[lo + pitch:lo + pitch + C, :]], axis=-1)
            lo = (nslab + 2 * k) * pitch
            him = jnp.concatenate([hb_ref[lo:lo + C, :], hb_ref[lo + pitch:lo + pitch + C, :]], axis=-1)
            cols = slice(2 * k * LANES, 2 * (k + 1) * LANES)
            y = y + _dot_nt(hre.astype(BF16), ctr_ref[:, cols]) - _dot_nt(him.astype(BF16), cti_ref[:, cols])
        y = y + s5d_ref[...] * u
        zg = _gelu_tanh(y)
        gl = _dot(zg.astype(BF16), wglu_ref[...]) + bglu_ref[...]
        mix_ref[rows, 0:S5_WIDTH] = zg * _sigmoid(gl)

        half = RET_HEADS * RET_DK // 2
        q1 = proj_ref[rows, Q0:Q0 + half]
        q2 = proj_ref[rows, Q0 + half:Q0 + 2 * half]
        k1 = proj_ref[rows, K0:K0 + half]
        k2 = proj_ref[rows, K0 + half:K0 + 2 * half]
        cs_ = cos_ref[pl.ds(ci * C, C), :]
        sn_ = sin_ref[pl.ds(ci * C, C), :]
        qr = jnp.concatenate([q1 * cs_ - q2 * sn_, q1 * sn_ + q2 * cs_], axis=-1)
        kr = jnp.concatenate([k1 * cs_ - k2 * sn_, k1 * sn_ + k2 * cs_], axis=-1) * (RET_DK ** -0.5)
        v = proj_ref[rows, V0:V0 + RET_WIDTH]
        gate = proj_ref[rows, G0:G0 + RET_WIDTH]
        s_prev = ret_out[bi]
        kb = kr.astype(BF16)
        o = _dot((qr * qw_ref[...]).astype(BF16), s_prev.astype(BF16))
        for h in range(RET_HEADS):
            sc = _dot_nt((qr * hmq_ref[h]).astype(BF16), kb) * dec_ref[h]
            o = o + _dot(sc.astype(BF16), (v * hmv_ref[h]).astype(BF16))
        kt = (kr * kw_ref[...]).T
        kv = _dot(kt.astype(BF16), v.astype(BF16))
        ret_out[bi] = s_prev * cdc_ref[...] + kv * mbd_ref[...]
        mavg = mavg_ref[...]
        mu = _dot(o.astype(BF16), mavg)
        dlt = o - mu
        var = _dot((dlt * dlt).astype(BF16), mavg)
        mix_ref[rows, S5_WIDTH:S5_WIDTH + RET_WIDTH] = _silu(gate) * (dlt * lax.rsqrt(var + EPS))

        z = proj_ref[rows, Z0:Z0 + SSD_WIDTH]
        xbc = proj_ref[rows, X0:X0 + SSD_CONV_DIM]
        xp_ref[0:SUBLANES, :] = conv_out[bi]
        xp_ref[SUBLANES:SUBLANES + C, :] = xbc
        cw = cw_ref[...]
        acc = cb_ref[...] + cw[3:4] * xbc
        for i in range(SSD_CONV - 1):
            lo = SUBLANES - (SSD_CONV - 1) + i
            acc = acc + cw[i:i + 1] * xp_ref[lo:lo + C, :]
        conv_out[bi] = xp_ref[C:C + SUBLANES, :]
        xc = _silu(acc)
        xs = xc[:, 0:SSD_WIDTH]
        ngl = SSD_DSTATE
        bm = xc[:, SSD_WIDTH:SSD_WIDTH + SSD_GROUPS * ngl]
        cm = xc[:, SSD_WIDTH + SSD_GROUPS * ngl:SSD_CONV_DIM]
        dt = _softplus(proj_ref[rows, DT0:DT0 + LANES] + dtb_ref[...])
        a_row = jnp.where(lane128 < SSD_HEADS, -jnp.exp(alog_ref[...]), 0.0)
        dta = dt * a_row
        cs3 = _dot(tri_ref[...], _split3(dta))
        cs = cs3[:, 0:LANES] + cs3[:, LANES:2 * LANES] + cs3[:, 2 * LANES:3 * LANES]
        dt_e = _dot(_split3(dt), e3_ref[...])
        cs_e = _dot(_split3(cs), e3_ref[...])
        last_e = cs_e[C - 1:C, :]
        ecs_e = jnp.exp(cs_e)
        wend_e = jnp.exp(last_e - cs_e)
        cdec_e = jnp.exp(last_e)
        xdt = xs * dt_e
        cst = cs.T
        hpg = SSD_HEADS // SSD_GROUPS
        gw = hpg * SSD_HEADDIM
        ys = []
        for g in range(SSD_GROUPS):
            cm_g = cm[:, g * ngl:(g + 1) * ngl].astype(BF16)
            bm_g = bm[:, g * ngl:(g + 1) * ngl]
            cbm = _dot_nt(cm_g, bm_g.astype(BF16))
            st = ssd_out[bi, g]
            xdt_g = xdt[:, g * gw:(g + 1) * gw]
            yg = _dot(cm_g, st.astype(BF16)) * ecs_e[:, g * gw:(g + 1) * gw]
            for hl in range(hpg):
                h = g * hpg + hl
                seg = cs[:, h:h + 1] - cst[h:h + 1, :]
                lm = jnp.where(causal, jnp.exp(seg), 0.0)
                yg = yg + _dot((cbm * lm).astype(BF16), (xdt_g * hmv_ref[hl]).astype(BF16))
            ys.append(yg)
            new = _dot(bm_g.T.astype(BF16), (xdt_g * wend_e[:, g * gw:(g + 1) * gw]).astype(BF16))
            ssd_out[bi, g] = st * cdec_e[:, g * gw:(g + 1) * gw] + new
        yss = jnp.concatenate(ys, axis=-1) + dskip_ref[...] * xs
        mix_ref[rows, S5_WIDTH + RET_WIDTH:D_MODEL] = _rms(yss * _silu(z), ng_ref[...])

    if nb == 1:
        for ci in range(nchunk):
            segment(0, ci, ci * C)
    else:
        def seg_body(j, carry):
            segment(j, 0, pl.multiple_of(j * C, C))
            return carry
        lax.fori_loop(0, nb, seg_body, 0)

    xo_ref[...] = x + _dot(mix_ref[...].astype(BF16), wout_ref[...])


def _const_spec(shape, l=None):
    if l is None:
        return pl.BlockSpec(shape, lambda b, c: (0,) * len(shape), pipeline_mode=pl.Buffered(1))
    nd = len(shape)
    return pl.BlockSpec((None,) + tuple(shape), lambda b, c: (l,) + (0,) * nd, pipeline_mode=pl.Buffered(1))


def _mixer_call(x2d, states, prm, tabs, l, *, batch, seq, nb, tile, chunk):
    C = chunk
    n_b = batch // nb
    n_t = seq // tile
    rows = nb * tile
    row_spec = pl.BlockSpec((rows, D_MODEL), lambda b, c: (b * n_t + c, 0))
    st_shapes = [(SUBLANES, LANES), (SUBLANES, LANES), (RET_HEADS * RET_DK, RET_WIDTH),
                 (SSD_GROUPS, SSD_DSTATE, SSD_WIDTH // SSD_GROUPS), (SUBLANES, SSD_CONV_DIM)]
    st_specs = [pl.BlockSpec((nb,) + s, lambda b, c, _n=len(s): (b,) + (0,) * _n) for s in st_shapes]
    in_specs = [
        row_spec,
        _const_spec((1, D_MODEL), l),
        _const_spec((D_MODEL, IN_PAD), l),
        pl.BlockSpec((tile, LANES), lambda b, c: (c, 0)),
        pl.BlockSpec((tile, LANES), lambda b, c: (c, 0)),
        _const_spec((SUBLANES, LANES), l), _const_spec((SUBLANES, LANES), l),
        _const_spec((S5_WIDTH, 2 * S5_LANES), l),
        _const_spec((S5_WIDTH, S5_LANES), l), _const_spec((S5_WIDTH, S5_LANES), l),
        _const_spec((1, S5_WIDTH), l), _const_spec((S5_WIDTH, S5_WIDTH), l), _const_spec((1, S5_WIDTH), l),
        _const_spec((RET_HEADS, C, C)), _const_spec((C, RET_WIDTH)), _const_spec((C, RET_WIDTH)),
        _const_spec((1, RET_WIDTH)), _const_spec((RET_WIDTH, RET_WIDTH)), _const_spec((RET_WIDTH, RET_WIDTH)),
        _const_spec((RET_HEADS, 1, RET_WIDTH)), _const_spec((RET_HEADS, 1, RET_WIDTH)),
        _const_spec((C, C)), _const_spec((3 * LANES, SSD_WIDTH)),
        _const_spec((SSD_CONV, SSD_CONV_DIM), l), _const_spec((1, SSD_CONV_DIM), l),
        _const_spec((1, LANES), l), _const_spec((1, LANES), l),
        _const_spec((1, SSD_WIDTH), l), _const_spec((1, SSD_WIDTH), l),
        _const_spec((D_MODEL, D_MODEL), l),
    ] + st_specs
    out_specs = [row_spec] + st_specs
    out_shape = [jax.ShapeDtypeStruct(x2d.shape, F32)] + [jax.ShapeDtypeStruct(s.shape, F32) for s in states]
    scratch = [pltpu.VMEM((rows, IN_PAD), F32), pltpu.VMEM((rows, D_MODEL), F32),
               pltpu.VMEM((2 * S5_LANES // LANES * (C + SUBLANES), LANES), F32),
               pltpu.VMEM((C + SUBLANES, SSD_CONV_DIM), F32)]
    kern = functools.partial(_mixer_kernel, nb=nb, tile=tile, chunk=C)
    outs = pl.pallas_call(
        kern, grid=(n_b, n_t), in_specs=in_specs, out_specs=out_specs, out_shape=out_shape,
        scratch_shapes=scratch, name=f"mixer_c{C}",
        compiler_params=pltpu.CompilerParams(dimension_semantics=("arbitrary", "arbitrary"),
                                             vmem_limit_bytes=VMEM_LIMIT),
    )(x2d, prm["g1"], prm["win"], tabs["cos"], tabs["sin"],
      prm["a8r"], prm["a8i"], prm["bb"], prm["ctr"], prm["cti"], prm["s5d"], prm["wglu"], prm["bglu"],
      tabs["dec"], tabs["qw"], tabs["kw"], tabs["cdc"], tabs["mbd"], tabs["mavg"], tabs["hmq"], tabs["hmv"],
      tabs["tri"], tabs["e3"],
      prm["cw"], prm["cb"], prm["dtb"], prm["alog"], prm["dskip"], prm["ng"], prm["wout"],
      *states)
    return outs[0], tuple(outs[1:])


def _ffn_kernel(x_ref, g2_ref, wg_ref, wu_ref, wd_ref, gf_ref, o_ref, *, final):
    x = x_ref[...]
    hn = _rms(x, g2_ref[...]).astype(BF16)
    acc = x
    nhalf = 2
    w = D_FF // nhalf
    for i in range(nhalf):
        gt = _dot(hn, wg_ref[:, i * w:(i + 1) * w])
        up = _dot(hn, wu_ref[:, i * w:(i + 1) * w])
        acc = acc + _dot((_silu(gt) * up).astype(BF16), wd_ref[i * w:(i + 1) * w, :])
    if final:
        acc = _rms(acc, gf_ref[...])
    o_ref[...] = acc


def _ffn_call(x2d, prm, l, *, rows, final):
    n = x2d.shape[0] // rows
    row_spec = pl.BlockSpec((rows, D_MODEL), lambda i: (i, 0))

    def wspec(shape, layer=True):
        nd = len(shape)
        if layer:
            return pl.BlockSpec((None,) + shape, lambda i: (l,) + (0,) * nd, pipeline_mode=pl.Buffered(1))
        return pl.BlockSpec(shape, lambda i: (0,) * nd, pipeline_mode=pl.Buffered(1))

    return pl.pallas_call(
        functools.partial(_ffn_kernel, final=final), grid=(n,),
        in_specs=[row_spec, wspec((1, D_MODEL)), wspec((D_MODEL, D_FF)), wspec((D_MODEL, D_FF)),
                  wspec((D_FF, D_MODEL)), wspec((1, D_MODEL), layer=False)],
        out_specs=row_spec, out_shape=jax.ShapeDtypeStruct(x2d.shape, F32),
        name="ffn_final" if final else "ffn",
        compiler_params=pltpu.CompilerParams(dimension_semantics=("arbitrary",), vmem_limit_bytes=VMEM_LIMIT),
    )(x2d, prm["g2"], prm["wg"], prm["wu"], prm["wd"], prm["gf"])


def _in_perm():
    idx = list(range(0, S5_WIDTH))
    for base in (S5_WIDTH, S5_WIDTH + RET_HEADS * RET_DK):
        for hf in range(2):
            for h in range(RET_HEADS):
                idx += [base + h * RET_DK + hf * (RET_DK // 2) + d for d in range(RET_DK // 2)]
    start = S5_WIDTH + 2 * RET_HEADS * RET_DK
    idx += list(range(start, start + 2 * RET_WIDTH + SSD_WIDTH + SSD_CONV_DIM + SSD_HEADS))
    return jnp.asarray(idx, jnp.int32)


def _prepare(norm1_g, w_in, s5_lam_re, s5_lam_im, s5_log_dt, s5_b_re, s5_b_im, s5_c_re, s5_c_im,
             s5_d, s5_w_glu, s5_b_glu, ssd_conv_w, ssd_conv_b, ssd_dt_bias, ssd_a_log, ssd_d, ssd_norm_g,
             w_out, norm2_g, w_gate, w_up, w_down, final_norm_g):
    depth = w_in.shape[0]
    win = jnp.take(w_in, _in_perm(), axis=2)
    win = jnp.pad(win, ((0, 0), (0, 0), (0, IN_PAD - win.shape[2]))).astype(BF16)
    a_re, a_im, bbr, bbi = _s5_params(s5_lam_re, s5_lam_im, s5_log_dt, s5_b_re, s5_b_im)
    rg = jnp.arange(S5_WIDTH)[:, None] // S5_GROUP_CH
    cg = jnp.arange(S5_LANES)[None, :] // S5_STATE
    blk = (rg == cg)[None]

    def bdiag_b(t):
        return jnp.where(blk, jnp.tile(t, (1, S5_GROUPS, 1)), 0.0)

    def bdiag_c(c):
        c2 = c.reshape(depth, S5_WIDTH, S5_STATE)
        return jnp.where(blk, jnp.tile(c2, (1, 1, S5_GROUPS)), 0.0)

    pad_l = ((0, 0), (0, 0), (0, LANES - SSD_HEADS))
    return {
        "g1": norm1_g[:, None, :], "win": win,
        "a8r": a_re.reshape(depth, SUBLANES, LANES), "a8i": a_im.reshape(depth, SUBLANES, LANES),
        "bb": jnp.concatenate([bdiag_b(bbr), bdiag_b(bbi)], axis=-1).astype(BF16),
        "ctr": bdiag_c(s5_c_re).astype(BF16), "cti": bdiag_c(s5_c_im).astype(BF16),
        "s5d": s5_d[:, None, :], "wglu": s5_w_glu.astype(BF16), "bglu": s5_b_glu[:, None, :],
        "cw": ssd_conv_w, "cb": ssd_conv_b[:, None, :],
        "dtb": jnp.pad(ssd_dt_bias[:, None, :], pad_l), "alog": jnp.pad(ssd_a_log[:, None, :], pad_l),
        "dskip": jnp.repeat(ssd_d, SSD_HEADDIM, axis=-1)[:, None, :], "ng": ssd_norm_g[:, None, :],
        "wout": w_out.astype(BF16), "g2": norm2_g[:, None, :],
        "wg": w_gate.astype(BF16), "wu": w_up.astype(BF16), "wd": w_down.astype(BF16),
        "gf": final_norm_g[None, :],
    }


def _tables(seq, chunk, t0):
    C = chunk
    lg = [math.log(1.0 - 2.0 ** (-5.0 - h)) for h in range(RET_HEADS)]
    lgv = jnp.asarray(lg, F32)
    i = jnp.arange(C, dtype=F32)
    rel = i[:, None] - i[None, :]
    dec = jnp.where(rel >= 0, jnp.exp(lgv[:, None, None] * jnp.maximum(rel, 0.0)), 0.0)
    lane = jnp.arange(RET_WIDTH)
    hq = (lane % (RET_WIDTH // 2)) // (RET_DK // 2)
    hv = lane // RET_DV
    qw = jnp.exp(lgv[hq][None, :] * (i + 1.0)[:, None])
    kw = jnp.exp(lgv[hq][None, :] * (C - 1.0 - i)[:, None])
    cdc = jnp.exp(lgv[hv] * C)[None, :]
    mbd = (hq[:, None] == hv[None, :]).astype(F32)
    mavg = jnp.where(hv[:, None] == hv[None, :], 1.0 / RET_DV, 0.0).astype(BF16)
    hmq = (hq[None, :] == jnp.arange(RET_HEADS)[:, None]).astype(F32)[:, None, :]
    hmv = (hv[None, :] == jnp.arange(RET_HEADS)[:, None]).astype(F32)[:, None, :]
    tri = (i[:, None] >= i[None, :]).astype(BF16)
    r = jnp.arange(3 * LANES) % LANES
    e3 = (r[:, None] == (jnp.arange(SSD_WIDTH) // SSD_HEADDIM)[None, :]).astype(BF16)
    half = RET_DK // 2
    inv_freq = ROPE_BASE ** (-jnp.arange(half, dtype=F32) / half)
    pos = (t0 + jnp.arange(seq)).astype(F32)
    ang = pos[:, None] * jnp.tile(inv_freq, RET_HEADS)[None, :]
    return {"dec": dec, "qw": qw, "kw": kw, "cdc": cdc, "mbd": mbd, "mavg": mavg, "hmq": hmq, "hmv": hmv,
            "tri": tri, "e3": e3, "cos": jnp.cos(ang), "sin": jnp.sin(ang)}


def _ret_to_kernel(s):
    b = s.shape[0]
    half = RET_DK // 2
    t = s.reshape(b, RET_HEADS, 2, half, RET_DV).transpose(0, 2, 1, 3, 4)
    eye = jnp.eye(RET_HEADS, dtype=bool)[None, None, :, None, :, None]
    full = jnp.where(eye, t[:, :, :, :, None, :], 0.0)
    return full.reshape(b, RET_HEADS * RET_DK, RET_WIDTH)


def _ret_from_kernel(s):
    b = s.shape[0]
    half = RET_DK // 2
    t = s.reshape(b, 2, RET_HEADS, half, RET_HEADS, RET_DV)
    d = jnp.diagonal(t, axis1=2, axis2=4)
    return d.transpose(0, 4, 1, 2, 3).reshape(b, RET_HEADS, RET_DK, RET_DV)


def _ssd_to_kernel(s):
    b = s.shape[0]
    hpg = SSD_HEADS // SSD_GROUPS
    t = s.reshape(b, SSD_GROUPS, hpg, SSD_HEADDIM, SSD_DSTATE).transpose(0, 1, 4, 2, 3)
    return t.reshape(b, SSD_GROUPS, SSD_DSTATE, hpg * SSD_HEADDIM)


def _ssd_from_kernel(s):
    b = s.shape[0]
    hpg = SSD_HEADS // SSD_GROUPS
    t = s.reshape(b, SSD_GROUPS, SSD_DSTATE, hpg, SSD_HEADDIM).transpose(0, 1, 3, 4, 2)
    return t.reshape(b, SSD_HEADS, SSD_HEADDIM, SSD_DSTATE)


def _plan(batch, seq):
    if seq >= 512:
        return 1, 512, 256
    nb = max(1, min(batch, 256 // seq))
    while batch % nb:
        nb -= 1
    return nb, seq, seq


def _trunk(x, states_in, prm, t0):
    batch, seq, _ = x.shape
    nb, tile, chunk = _plan(batch, seq)
    tabs = _tables(seq, chunk, t0)
    depth = prm["win"].shape[0]
    x2d = x.reshape(batch * seq, D_MODEL)
    ffn_rows = min(512, batch * seq)
    outs = []
    for l in range(depth):
        st = states_in[l]
        x2d, st_out = _mixer_call(x2d, st, prm, tabs, l, batch=batch, seq=seq, nb=nb, tile=tile, chunk=chunk)
        x2d = _ffn_call(x2d, prm, l, rows=ffn_rows, final=(l == depth - 1))
        outs.append(st_out)
    return x2d.reshape(batch, seq, D_MODEL), outs


def _zero_states(batch):
    return (jnp.zeros((batch, SUBLANES, LANES), F32), jnp.zeros((batch, SUBLANES, LANES), F32),
            jnp.zeros((batch, RET_HEADS * RET_DK, RET_WIDTH), F32),
            jnp.zeros((batch, SSD_GROUPS, SSD_DSTATE, SSD_WIDTH // SSD_GROUPS), F32),
            jnp.zeros((batch, SUBLANES, SSD_CONV_DIM), F32))


def _states_out(outs):
    s5r = jnp.stack([o[0].reshape(-1, S5_GROUPS, S5_STATE) for o in outs])
    s5i = jnp.stack([o[1].reshape(-1, S5_GROUPS, S5_STATE) for o in outs])
    ret = jnp.stack([_ret_from_kernel(o[2]) for o in outs])
    ssd = jnp.stack([_ssd_from_kernel(o[3]) for o in outs])
    conv = jnp.stack([o[4][:, SUBLANES - (SSD_CONV - 1):, :] for o in outs])
    return s5r, s5i, ret, ssd, conv


def kernel(x_prompt, x_sample, state_s5_re, state_s5_im, state_ret, state_ssd, cache_ssd_conv, norm1_g, w_in, s5_lam_re, s5_lam_im, s5_log_dt, s5_b_re, s5_b_im, s5_c_re, s5_c_im, s5_d, s5_w_glu, s5_b_glu, ssd_conv_w, ssd_conv_b, ssd_dt_bias, ssd_a_log, ssd_d, ssd_norm_g, w_out, norm2_g, w_gate, w_up, w_down, final_norm_g):
    prm = _prepare(norm1_g, w_in, s5_lam_re, s5_lam_im, s5_log_dt, s5_b_re, s5_b_im, s5_c_re, s5_c_im,
                   s5_d, s5_w_glu, s5_b_glu, ssd_conv_w, ssd_conv_b, ssd_dt_bias, ssd_a_log, ssd_d, ssd_norm_g,
                   w_out, norm2_g, w_gate, w_up, w_down, final_norm_g)
    depth = w_in.shape[0]
    past_len = 1024

    bp = x_prompt.shape[0]
    y_prompt, p_outs = _trunk(x_prompt, [_zero_states(bp)] * depth, prm, 0)

    bs = x_sample.shape[0]
    pad_conv = ((0, 0), (SUBLANES - (SSD_CONV - 1), 0), (0, 0))
    s_in = [(state_s5_re[l].reshape(bs, SUBLANES, LANES), state_s5_im[l].reshape(bs, SUBLANES, LANES),
             _ret_to_kernel(state_ret[l]), _ssd_to_kernel(state_ssd[l]), jnp.pad(cache_ssd_conv[l], pad_conv))
            for l in range(depth)]
    y_sample, s_outs = _trunk(x_sample, s_in, prm, past_len)

    return (y_prompt, y_sample) + _states_out(p_outs) + _states_out(s_outs)
```

```python
import functools
import math

import jax
import jax.numpy as jnp
from jax import lax
from jax.experimental import pallas as pl
from jax.experimental.pallas import tpu as pltpu

F32 = jnp.float32
BF16 = jnp.bfloat16

D_MODEL = 1024
EPS = 1e-6
ROPE_BASE = 10000.0
S5_WIDTH = 256
S5_GROUPS = 16
S5_GROUP_CH = 16
S5_STATE = 64
S5_LANES = S5_GROUPS * S5_STATE
RET_HEADS = 4
RET_DK = 64
RET_DV = 64
RET_WIDTH = RET_HEADS * RET_DV
SSD_WIDTH = 512
SSD_HEADDIM = 64
SSD_HEADS = 8
SSD_GROUPS = 2
SSD_DSTATE = 128
SSD_CONV = 4
SSD_CONV_DIM = SSD_WIDTH + 2 * SSD_GROUPS * SSD_DSTATE
D_FF = 2816
LANES = 128
SUBLANES = 8
MXU_DEPTH = 256

U0 = 0
Q0 = U0 + S5_WIDTH
K0 = Q0 + RET_HEADS * RET_DK
V0 = K0 + RET_HEADS * RET_DK
G0 = V0 + RET_WIDTH
Z0 = G0 + RET_WIDTH
X0 = Z0 + SSD_WIDTH
DT0 = X0 + SSD_CONV_DIM
IN_PAD = DT0 + LANES

CONV_TAIL = SSD_CONV - 1
CONV_LO = SUBLANES - CONV_TAIL

VMEM_LIMIT = 60000 * 1024


def _dot(a, b):
    return jnp.dot(a, b, preferred_element_type=F32)


def _dot_nt(a, b):
    return lax.dot_general(a, b, (((1,), (1,)), ((), ())), preferred_element_type=F32)


def _sigmoid(x):
    return 1.0 / (1.0 + jnp.exp(-x))


def _silu(x):
    return x * _sigmoid(x)


def _gelu_tanh(x):
    c = math.sqrt(2.0 / math.pi)
    return x * (0.5 * (1.0 + jnp.tanh(c * (x + 0.044715 * (x * x * x)))))


def _softplus(x):
    return jnp.maximum(x, 0.0) + jnp.log1p(jnp.exp(-jnp.abs(x)))


def _rms(x, g):
    ms = jnp.mean(x * x, axis=-1, keepdims=True)
    return x * lax.rsqrt(ms + EPS) * g


def _split3(x):
    hi = x.astype(BF16)
    r1 = x - hi.astype(F32)
    mid = r1.astype(BF16)
    lo = (r1 - mid.astype(F32)).astype(BF16)
    return jnp.concatenate([hi, mid, lo], axis=-1)


def _s5_param_kernel(lr_ref, li_ref, ldt_ref, br_ref, bi_ref, ar_ref, ai_ref, bbr_ref, bbi_ref):
    depth = lr_ref.shape[0]
    for l in range(depth):
        lr = lr_ref[l:l + 1, :]
        li = li_ref[l:l + 1, :]
        dt = jnp.exp(ldt_ref[l:l + 1, :])
        mag = jnp.exp(lr * dt)
        ar = mag * jnp.cos(li * dt)
        ai = mag * jnp.sin(li * dt)
        den = lr * lr + li * li
        nr = ar - 1.0
        ni = ai
        kr = (nr * lr + ni * li) / den
        ki = (ni * lr - nr * li) / den
        br = br_ref[l]
        bi = bi_ref[l]
        bbr_ref[l] = kr * br - ki * bi
        bbi_ref[l] = kr * bi + ki * br
        ar_ref[l:l + 1, :] = ar
        ai_ref[l:l + 1, :] = ai


def _s5_params(lam_re, lam_im, log_dt, b_re, b_im):
    depth = lam_re.shape[0]
    lr = lam_re.reshape(depth, S5_LANES)
    li = lam_im.reshape(depth, S5_LANES)
    ldt = jnp.repeat(log_dt, S5_STATE, axis=-1)
    brt = jnp.transpose(b_re, (0, 3, 1, 2)).reshape(depth, S5_GROUP_CH, S5_LANES)
    bit = jnp.transpose(b_im, (0, 3, 1, 2)).reshape(depth, S5_GROUP_CH, S5_LANES)
    out_shape = (jax.ShapeDtypeStruct((depth, S5_LANES), F32),
                 jax.ShapeDtypeStruct((depth, S5_LANES), F32),
                 jax.ShapeDtypeStruct((depth, S5_GROUP_CH, S5_LANES), F32),
                 jax.ShapeDtypeStruct((depth, S5_GROUP_CH, S5_LANES), F32))
    return pl.pallas_call(_s5_param_kernel, out_shape=out_shape, name="s5_params")(lr, li, ldt, brt, bit)


def _round_robin(chains):
    chains = list(chains)
    while chains:
        for c in list(chains):
            if next(c, StopIteration) is StopIteration:
                chains.remove(c)


def _mixer_kernel(*refs, nb, tile, chunk, n_t, zero_init):
    (x_ref, g1_ref, win_ref, cos_ref, sin_ref,
     a8r_ref, a8i_ref, bb_ref, ctr_ref, cti_ref, s5d_ref, wglu_ref, bglu_ref,
     dec_ref, qw_ref, kw_ref, cdc_ref, mbd_ref, mavg_ref, hmq_ref, hmv_ref, tri_ref, e3_ref,
     cw_ref, cb_ref, dtb_ref, alog_ref, dskip_ref, ng_ref, wout_ref) = refs[:30]
    refs = refs[30:]
    if not zero_init:
        s5r_in, s5i_in, ret_in, ssd_in, conv_in = refs[:5]
        refs = refs[5:]
    (xo_ref, s5r_out, s5i_out, ret_out, ssd_out, conv_out,
     proj_ref, mix_ref, hb_ref, xp_ref, ret_sc, ssd_sc, conv_sc) = refs

    C = chunk
    nchunk = tile // C
    ic = pl.program_id(0) % n_t if n_t > 1 else 0
    half = RET_HEADS * RET_DK // 2
    hpg = SSD_HEADS // SSD_GROUPS
    gw = hpg * SSD_HEADDIM
    ret_blocks = [(h, hf, hf * half + h * (RET_DK // 2), h * RET_DV)
                  for h in range(RET_HEADS) for hf in range(2)]

    def init_states():
        if zero_init:
            s5r_out[...] = jnp.zeros_like(s5r_out)
            s5i_out[...] = jnp.zeros_like(s5i_out)
            ret_sc[...] = jnp.zeros_like(ret_sc)
            ssd_sc[...] = jnp.zeros_like(ssd_sc)
            conv_sc[...] = jnp.zeros_like(conv_sc)
            return
        s5r_out[...] = s5r_in[...]
        s5i_out[...] = s5i_in[...]
        ret_sc[...] = jnp.zeros_like(ret_sc)
        conv_sc[...] = jnp.zeros_like(conv_sc)
        for b in range(nb):
            for h, hf, r0, c0 in ret_blocks:
                ret_sc[b, r0:r0 + RET_DK // 2, c0:c0 + RET_DV] = ret_in[b, h, hf * (RET_DK // 2):(hf + 1) * (RET_DK // 2), :]
            for g in range(SSD_GROUPS):
                ssd_sc[b, g] = ssd_in[b, g * hpg:(g + 1) * hpg].reshape(gw, SSD_DSTATE).T
            conv_sc[b, CONV_LO:SUBLANES, :] = conv_in[b]

    def final_states():
        for b in range(nb):
            for h, hf, r0, c0 in ret_blocks:
                ret_out[b, h, hf * (RET_DK // 2):(hf + 1) * (RET_DK // 2), :] = ret_sc[b, r0:r0 + RET_DK // 2, c0:c0 + RET_DV]
            for g in range(SSD_GROUPS):
                ssd_out[b, g * hpg:(g + 1) * hpg] = ssd_sc[b, g].T.reshape(hpg, SSD_HEADDIM, SSD_DSTATE)
            conv_out[b] = conv_sc[b, CONV_LO:SUBLANES, :]

    if n_t > 1:
        pl.when(ic == 0)(init_states)
    else:
        init_states()

    x = x_ref[...]
    proj_ref[...] = _dot(_rms(x, g1_ref[...]).astype(BF16), win_ref[...])

    causal = (lax.broadcasted_iota(jnp.int32, (C, C), 0) >= lax.broadcasted_iota(jnp.int32, (C, C), 1))
    lane128 = lax.broadcasted_iota(jnp.int32, (1, LANES), 1)

    def s5_chain(bi, ci, r0):
        rows = pl.ds(r0, C)
        u = proj_ref[rows, U0:U0 + S5_WIDTH]
        bu = _dot(u.astype(BF16), bb_ref[...])
        nslab = S5_LANES // LANES
        pitch = C + SUBLANES
        for s in range(2 * nslab):
            hb_ref[s * pitch:s * pitch + C, :] = bu[:, s * LANES:(s + 1) * LANES]
        yield
        ar = a8r_ref[...]
        ai = a8i_ref[...]
        hr = s5r_out[bi]
        hi = s5i_out[bi]
        for t in range(C):
            ld_r = pl.ds(t, nslab, stride=pitch)
            ld_i = pl.ds(nslab * pitch + t, nslab, stride=pitch)
            hr, hi = (ar * hr - ai * hi + hb_ref[ld_r, :], ar * hi + ai * hr + hb_ref[ld_i, :])
            hb_ref[ld_r, :] = hr
            hb_ref[ld_i, :] = hi
        s5r_out[bi] = hr
        s5i_out[bi] = hi
        yield
        y = s5d_ref[...] * u
        for k in range(nslab // 2):
            lo = 2 * k * pitch
            hre = jnp.concatenate([hb_ref[lo:lo + C, :], hb_ref[lo + pitch:lo + pitch + C, :]], axis=-1)
            lo = (nslab + 2 * k) * pitch
            him = jnp.concatenate([hb_ref[lo:lo + C, :], hb_ref[lo + pitch:lo + pitch + C, :]], axis=-1)
            cols = slice(2 * k * LANES, 2 * (k + 1) * LANES)
            y = y + _dot_nt(hre.astype(BF16), ctr_ref[:, cols]) - _dot_nt(him.astype(BF16), cti_ref[:, cols])
            yield
        zg = _gelu_tanh(y)
        gl = _dot(zg.astype(BF16), wglu_ref[...]) + bglu_ref[...]
        yield
        mix_ref[rows, 0:S5_WIDTH] = zg * _sigmoid(gl)

    def ret_chain(bi, ci, r0):
        rows = pl.ds(r0, C)
        q1 = proj_ref[rows, Q0:Q0 + half]
        q2 = proj_ref[rows, Q0 + half:Q0 + 2 * half]
        k1 = proj_ref[rows, K0:K0 + half]
        k2 = proj_ref[rows, K0 + half:K0 + 2 * half]
        cs_ = cos_ref[pl.ds(ci * C, C), :]
        sn_ = sin_ref[pl.ds(ci * C, C), :]
        qr = jnp.concatenate([q1 * cs_ - q2 * sn_, q1 * sn_ + q2 * cs_], axis=-1)
        kr = jnp.concatenate([k1 * cs_ - k2 * sn_, k1 * sn_ + k2 * cs_], axis=-1) * (RET_DK ** -0.5)
        gate = proj_ref[rows, G0:G0 + RET_WIDTH]
        s_prev = ret_sc[bi]
        qb = qr.astype(BF16)
        kb = kr.astype(BF16)
        vb = proj_ref[rows, V0:V0 + RET_WIDTH].astype(BF16)
        o = _dot((qr * qw_ref[...]).astype(BF16), s_prev.astype(BF16))
        yield
        scs = []
        for h in range(RET_HEADS):
            scs.append(_dot_nt(qb * hmq_ref[h], kb) * dec_ref[h])
            yield
        kt = (kr * kw_ref[...]).T
        kv = _dot(kt.astype(BF16), vb)
        ret_sc[bi] = s_prev * cdc_ref[...] + kv * mbd_ref[...]
        yield
        for h in range(RET_HEADS):
            o = o + _dot(scs[h].astype(BF16), vb * hmv_ref[h])
            yield
        mavg = mavg_ref[...]
        mu = _dot(o.astype(BF16), mavg)
        yield
        dlt = o - mu
        var = _dot((dlt * dlt).astype(BF16), mavg)
        yield
        mix_ref[rows, S5_WIDTH:S5_WIDTH + RET_WIDTH] = _silu(gate) * (dlt * lax.rsqrt(var + EPS))

    def ssd_chain(bi, ci, r0):
        rows = pl.ds(r0, C)
        z = proj_ref[rows, Z0:Z0 + SSD_WIDTH]
        xbc = proj_ref[rows, X0:X0 + SSD_CONV_DIM]
        xp_ref[0:SUBLANES, :] = conv_sc[bi]
        xp_ref[SUBLANES:SUBLANES + C, :] = xbc
        cw = cw_ref[...]
        acc = cb_ref[...] + cw[CONV_TAIL:SSD_CONV] * xbc
        for i in range(CONV_TAIL):
            acc = acc + cw[i:i + 1] * xp_ref[CONV_LO + i:CONV_LO + i + C, :]
        conv_sc[bi] = xp_ref[C:C + SUBLANES, :]
        xc = _silu(acc)
        xs = xc[:, 0:SSD_WIDTH]
        ngl = SSD_DSTATE
        bm = xc[:, SSD_WIDTH:SSD_WIDTH + SSD_GROUPS * ngl]
        cm = xc[:, SSD_WIDTH + SSD_GROUPS * ngl:SSD_CONV_DIM]
        dt = _softplus(proj_ref[rows, DT0:DT0 + LANES] + dtb_ref[...])
        a_row = jnp.where(lane128 < SSD_HEADS, -jnp.exp(alog_ref[...]), 0.0)
        dta = dt * a_row
        cs3 = _dot(tri_ref[...], _split3(dta))
        yield
        cs = cs3[:, 0:LANES] + cs3[:, LANES:2 * LANES] + cs3[:, 2 * LANES:3 * LANES]
        dt_e = _dot(_split3(dt), e3_ref[...])
        yield
        cs_e = _dot(_split3(cs), e3_ref[...])
        yield
        last_e = cs_e[C - 1:C, :]
        ecs_e = jnp.exp(cs_e)
        wend_e = jnp.exp(last_e - cs_e)
        cdec_e = jnp.exp(last_e)
        xdt = xs * dt_e
        cst = cs.T
        ys = []
        for g in range(SSD_GROUPS):
            cm_g = cm[:, g * ngl:(g + 1) * ngl].astype(BF16)
            bm_g = bm[:, g * ngl:(g + 1) * ngl]
            cbm = _dot_nt(cm_g, bm_g.astype(BF16))
            yield
            st = ssd_sc[bi, g]
            xdt_g = xdt[:, g * gw:(g + 1) * gw]
            xdt_b = xdt_g.astype(BF16)
            yg = _dot(cm_g, st.astype(BF16)) * ecs_e[:, g * gw:(g + 1) * gw]
            yield
            new = _dot(bm_g.T.astype(BF16), (xdt_g * wend_e[:, g * gw:(g + 1) * gw]).astype(BF16))
            ssd_sc[bi, g] = st * cdec_e[:, g * gw:(g + 1) * gw] + new
            yield
            for hl in range(hpg):
                h = g * hpg + hl
                seg = cs[:, h:h + 1] - cst[h:h + 1, :]
                lm = jnp.where(causal, jnp.exp(seg), 0.0)
                yg = yg + _dot((cbm * lm).astype(BF16), xdt_b * hmv_ref[hl])
                yield
            ys.append(yg)
        yss = jnp.concatenate(ys, axis=-1) + dskip_ref[...] * xs
        mix_ref[rows, S5_WIDTH + RET_WIDTH:D_MODEL] = _rms(yss * _silu(z), ng_ref[...])

    def out_chain(rows):
        mixb = mix_ref[rows, :].astype(BF16)
        for lo in range(0, D_MODEL, D_MODEL // 2):
            cols = slice(lo, lo + D_MODEL // 2)
            xo_ref[rows, cols] = x_ref[rows, cols] + _dot(mixb, wout_ref[:, cols])
            yield

    def head_groups(bi, ci, r0):
        return [s5_chain(bi, ci, r0), ret_chain(bi, ci, r0), ssd_chain(bi, ci, r0)]

    if nb == 1:
        prev = []
        for ci in range(nchunk):
            _round_robin(head_groups(0, ci, ci * C) + prev)
            prev = [out_chain(pl.ds(ci * C, C))]
        _round_robin(prev)
    else:
        def seg_body(j, carry):
            _round_robin(head_groups(j, 0, pl.multiple_of(j * C, C)))
            return carry
        lax.fori_loop(0, nb, seg_body, 0)
        _round_robin([out_chain(pl.ds(0, nb * C))])

    if n_t > 1:
        pl.when(ic == n_t - 1)(final_states)
    else:
        final_states()


def _const_spec(shape, l=None):
    if l is None:
        return pl.BlockSpec(shape, lambda k: (0,) * len(shape), pipeline_mode=pl.Buffered(1))
    nd = len(shape)
    return pl.BlockSpec((None,) + tuple(shape), lambda k: (l,) + (0,) * nd, pipeline_mode=pl.Buffered(1))


def _mixer_call(x2d, states, prm, tabs, l, *, batch, seq, nb, tile, chunk):
    C = chunk
    n_t = seq // tile
    total = (batch // nb) * n_t
    rows = nb * tile
    zero_init = states is None
    row_spec = pl.BlockSpec((rows, D_MODEL), lambda k: (k, 0))
    rope_spec = pl.BlockSpec((tile, LANES), lambda k: (k % n_t, 0))
    st_shapes = [(SUBLANES, LANES), (SUBLANES, LANES), (RET_HEADS, RET_DK, RET_DV),
                 (SSD_HEADS, SSD_HEADDIM, SSD_DSTATE), (CONV_TAIL, SSD_CONV_DIM)]
    st_out_specs = [pl.BlockSpec((nb,) + s, lambda k, _n=len(s): (k // n_t,) + (0,) * _n) for s in st_shapes]
    st_in_specs = [pl.BlockSpec((None, nb) + s, lambda k, _n=len(s): (l, k // n_t) + (0,) * _n) for s in st_shapes]
    in_specs = [
        row_spec,
        _const_spec((1, D_MODEL), l),
        _const_spec((D_MODEL, IN_PAD), l),
        rope_spec,
        rope_spec,
        _const_spec((SUBLANES, LANES), l), _const_spec((SUBLANES, LANES), l),
        _const_spec((S5_WIDTH, 2 * S5_LANES), l),
        _const_spec((S5_WIDTH, S5_LANES), l), _const_spec((S5_WIDTH, S5_LANES), l),
        _const_spec((1, S5_WIDTH), l), _const_spec((S5_WIDTH, S5_WIDTH), l), _const_spec((1, S5_WIDTH), l),
        _const_spec((RET_HEADS, C, C)), _const_spec((C, RET_WIDTH)), _const_spec((C, RET_WIDTH)),
        _const_spec((1, RET_WIDTH)), _const_spec((RET_WIDTH, RET_WIDTH)), _const_spec((RET_WIDTH, RET_WIDTH)),
        _const_spec((RET_HEADS, C, RET_WIDTH)), _const_spec((RET_HEADS, C, RET_WIDTH)),
        _const_spec((C, C)), _const_spec((3 * LANES, SSD_WIDTH)),
        _const_spec((SSD_CONV, SSD_CONV_DIM), l), _const_spec((1, SSD_CONV_DIM), l),
        _const_spec((1, LANES), l), _const_spec((1, LANES), l),
        _const_spec((1, SSD_WIDTH), l), _const_spec((1, SSD_WIDTH), l),
        _const_spec((D_MODEL, D_MODEL), l),
    ] + ([] if zero_init else st_in_specs)
    out_specs = [row_spec] + st_out_specs
    out_shape = ([jax.ShapeDtypeStruct(x2d.shape, F32)]
                 + [jax.ShapeDtypeStruct((batch,) + s, F32) for s in st_shapes])
    scratch = [pltpu.VMEM((rows, IN_PAD), F32), pltpu.VMEM((rows, D_MODEL), F32),
               pltpu.VMEM((2 * S5_LANES // LANES * (C + SUBLANES), LANES), F32),
               pltpu.VMEM((C + SUBLANES, SSD_CONV_DIM), F32),
               pltpu.VMEM((nb, RET_HEADS * RET_DK, RET_WIDTH), F32),
               pltpu.VMEM((nb, SSD_GROUPS, SSD_DSTATE, SSD_WIDTH // SSD_GROUPS), F32),
               pltpu.VMEM((nb, SUBLANES, SSD_CONV_DIM), F32)]
    kern = functools.partial(_mixer_kernel, nb=nb, tile=tile, chunk=C, n_t=n_t, zero_init=zero_init)
    outs = pl.pallas_call(
        kern, grid=(total,), in_specs=in_specs, out_specs=out_specs, out_shape=out_shape,
        scratch_shapes=scratch, name=f"mixer_c{C}",
        compiler_params=pltpu.CompilerParams(dimension_semantics=("arbitrary",),
                                             vmem_limit_bytes=VMEM_LIMIT),
    )(x2d, prm["g1"], prm["win"], tabs["cos"], tabs["sin"],
      prm["a8r"], prm["a8i"], prm["bb"], prm["ctr"], prm["cti"], prm["s5d"], prm["wglu"], prm["bglu"],
      tabs["dec"], tabs["qw"], tabs["kw"], tabs["cdc"], tabs["mbd"], tabs["mavg"], tabs["hmq"], tabs["hmv"],
      tabs["tri"], tabs["e3"],
      prm["cw"], prm["cb"], prm["dtb"], prm["alog"], prm["dskip"], prm["ng"], prm["wout"],
      *(() if zero_init else states))
    return outs[0], tuple(outs[1:])


def _ffn_kernel(x_ref, g2_ref, wg_ref, wu_ref, wd_ref, gf_ref, o_ref, *, final):
    x = x_ref[...]
    hn = _rms(x, g2_ref[...]).astype(BF16)
    acc = x
    split = (D_FF // (2 * MXU_DEPTH) + 1) * MXU_DEPTH
    for lo, hi in ((0, split), (split, D_FF)):
        gt = _dot(hn, wg_ref[:, lo:hi])
        up = _dot(hn, wu_ref[:, lo:hi])
        acc = acc + _dot((_silu(gt) * up).astype(BF16), wd_ref[lo:hi, :])
    if final:
        acc = _rms(acc, gf_ref[...])
    o_ref[...] = acc


def _ffn_call(x2d, prm, l, *, rows, final):
    n = x2d.shape[0] // rows
    row_spec = pl.BlockSpec((rows, D_MODEL), lambda i: (i, 0))

    def wspec(shape, layer=True):
        nd = len(shape)
        if layer:
            return pl.BlockSpec((None,) + shape, lambda i: (l,) + (0,) * nd, pipeline_mode=pl.Buffered(1))
        return pl.BlockSpec(shape, lambda i: (0,) * nd, pipeline_mode=pl.Buffered(1))

    return pl.pallas_call(
        functools.partial(_ffn_kernel, final=final), grid=(n,),
        in_specs=[row_spec, wspec((1, D_MODEL)), wspec((D_MODEL, D_FF)), wspec((D_MODEL, D_FF)),
                  wspec((D_FF, D_MODEL)), wspec((1, D_MODEL), layer=False)],
        out_specs=row_spec, out_shape=jax.ShapeDtypeStruct(x2d.shape, F32),
        name="ffn_final" if final else "ffn",
        compiler_params=pltpu.CompilerParams(dimension_semantics=("arbitrary",), vmem_limit_bytes=VMEM_LIMIT),
    )(x2d, prm["g2"], prm["wg"], prm["wu"], prm["wd"], prm["gf"])


def _reorder_w_in(w_in):
    depth = w_in.shape[0]
    qk = RET_HEADS * RET_DK

    def halves_first(w):
        w = w.reshape(depth, D_MODEL, RET_HEADS, 2, RET_DK // 2)
        return w.transpose(0, 1, 3, 2, 4).reshape(depth, D_MODEL, qk)

    wb = w_in.astype(BF16)
    parts = [wb[:, :, :S5_WIDTH], halves_first(wb[:, :, S5_WIDTH:S5_WIDTH + qk]),
             halves_first(wb[:, :, S5_WIDTH + qk:S5_WIDTH + 2 * qk]), wb[:, :, S5_WIDTH + 2 * qk:]]
    used = sum(p.shape[2] for p in parts)
    parts.append(jnp.zeros((depth, D_MODEL, IN_PAD - used), BF16))
    return jnp.concatenate(parts, axis=2)


def _prepare(norm1_g, w_in, s5_lam_re, s5_lam_im, s5_log_dt, s5_b_re, s5_b_im, s5_c_re, s5_c_im,
             s5_d, s5_w_glu, s5_b_glu, ssd_conv_w, ssd_conv_b, ssd_dt_bias, ssd_a_log, ssd_d, ssd_norm_g,
             w_out, norm2_g, w_gate, w_up, w_down, final_norm_g):
    depth = w_in.shape[0]
    a_re, a_im, bbr, bbi = _s5_params(s5_lam_re, s5_lam_im, s5_log_dt, s5_b_re, s5_b_im)
    rg = jnp.arange(S5_WIDTH)[:, None] // S5_GROUP_CH
    cg = jnp.arange(S5_LANES)[None, :] // S5_STATE
    blk = (rg == cg)[None]

    def bdiag_b(t):
        return jnp.where(blk, jnp.tile(t, (1, S5_GROUPS, 1)), 0.0)

    def bdiag_c(c):
        c2 = c.reshape(depth, S5_WIDTH, S5_STATE)
        return jnp.where(blk, jnp.tile(c2, (1, 1, S5_GROUPS)), 0.0)

    pad_l = ((0, 0), (0, 0), (0, LANES - SSD_HEADS))
    return {
        "g1": norm1_g[:, None, :], "win": _reorder_w_in(w_in),
        "a8r": a_re.reshape(depth, SUBLANES, LANES), "a8i": a_im.reshape(depth, SUBLANES, LANES),
        "bb": jnp.concatenate([bdiag_b(bbr), bdiag_b(bbi)], axis=-1).astype(BF16),
        "ctr": bdiag_c(s5_c_re).astype(BF16), "cti": bdiag_c(s5_c_im).astype(BF16),
        "s5d": s5_d[:, None, :], "wglu": s5_w_glu.astype(BF16), "bglu": s5_b_glu[:, None, :],
        "cw": ssd_conv_w, "cb": ssd_conv_b[:, None, :],
        "dtb": jnp.pad(ssd_dt_bias[:, None, :], pad_l), "alog": jnp.pad(ssd_a_log[:, None, :], pad_l),
        "dskip": jnp.repeat(ssd_d, SSD_HEADDIM, axis=-1)[:, None, :], "ng": ssd_norm_g[:, None, :],
        "wout": w_out.astype(BF16), "g2": norm2_g[:, None, :],
        "wg": w_gate.astype(BF16), "wu": w_up.astype(BF16), "wd": w_down.astype(BF16),
        "gf": final_norm_g[None, :],
    }


def _tables(seq, chunk, t0):
    C = chunk
    lg = [math.log(1.0 - 2.0 ** (-5.0 - h)) for h in range(RET_HEADS)]
    lgv = jnp.asarray(lg, F32)
    i = jnp.arange(C, dtype=F32)
    rel = i[:, None] - i[None, :]
    dec = jnp.where(rel >= 0, jnp.exp(lgv[:, None, None] * jnp.maximum(rel, 0.0)), 0.0)
    lane = jnp.arange(RET_WIDTH)
    hq = (lane % (RET_WIDTH // 2)) // (RET_DK // 2)
    hv = lane // RET_DV
    qw = jnp.exp(lgv[hq][None, :] * (i + 1.0)[:, None])
    kw = jnp.exp(lgv[hq][None, :] * (C - 1.0 - i)[:, None])
    cdc = jnp.exp(lgv[hv] * C)[None, :]
    mbd = (hq[:, None] == hv[None, :]).astype(F32)
    mavg = jnp.where(hv[:, None] == hv[None, :], 1.0 / RET_DV, 0.0).astype(BF16)
    heads = jnp.arange(RET_HEADS)[:, None, None]
    hmq = jnp.broadcast_to(hq[None, None, :] == heads, (RET_HEADS, C, RET_WIDTH)).astype(BF16)
    hmv = jnp.broadcast_to(hv[None, None, :] == heads, (RET_HEADS, C, RET_WIDTH)).astype(BF16)
    tri = (i[:, None] >= i[None, :]).astype(BF16)
    r = jnp.arange(3 * LANES) % LANES
    e3 = (r[:, None] == (jnp.arange(SSD_WIDTH) // SSD_HEADDIM)[None, :]).astype(BF16)
    half = RET_DK // 2
    inv_freq = ROPE_BASE ** (-jnp.arange(half, dtype=F32) / half)
    pos = (t0 + jnp.arange(seq)).astype(F32)
    ang = pos[:, None] * jnp.tile(inv_freq, RET_HEADS)[None, :]
    return {"dec": dec, "qw": qw, "kw": kw, "cdc": cdc, "mbd": mbd, "mavg": mavg, "hmq": hmq, "hmv": hmv,
            "tri": tri, "e3": e3, "cos": jnp.cos(ang), "sin": jnp.sin(ang)}


def _plan(batch, seq):
    if seq >= 512:
        return 1, 512, 256
    nb = max(1, min(batch, 256 // seq))
    while batch % nb:
        nb -= 1
    return nb, seq, seq


def _trunk(x, states, prm, t0):
    batch, seq, _ = x.shape
    nb, tile, chunk = _plan(batch, seq)
    tabs = _tables(seq, chunk, t0)
    depth = prm["win"].shape[0]
    x2d = x.reshape(batch * seq, D_MODEL)
    ffn_rows = min(512, batch * seq)
    outs = []
    for l in range(depth):
        x2d, st_out = _mixer_call(x2d, states, prm, tabs, l, batch=batch, seq=seq, nb=nb, tile=tile, chunk=chunk)
        x2d = _ffn_call(x2d, prm, l, rows=ffn_rows, final=(l == depth - 1))
        outs.append(st_out)
    stacked = [jnp.stack([o[i] for o in outs]) for i in range(5)]
    s5_shape = (depth, batch, S5_GROUPS, S5_STATE)
    return (x2d.reshape(batch, seq, D_MODEL),
            (stacked[0].reshape(s5_shape), stacked[1].reshape(s5_shape), stacked[2], stacked[3], stacked[4]))


def kernel(x_prompt, x_sample, state_s5_re, state_s5_im, state_ret, state_ssd, cache_ssd_conv, norm1_g, w_in, s5_lam_re, s5_lam_im, s5_log_dt, s5_b_re, s5_b_im, s5_c_re, s5_c_im, s5_d, s5_w_glu, s5_b_glu, ssd_conv_w, ssd_conv_b, ssd_dt_bias, ssd_a_log, ssd_d, ssd_norm_g, w_out, norm2_g, w_gate, w_up, w_down, final_norm_g):
    prm = _prepare(norm1_g, w_in, s5_lam_re, s5_lam_im, s5_log_dt, s5_b_re, s5_b_im, s5_c_re, s5_c_im,
                   s5_d, s5_w_glu, s5_b_glu, ssd_conv_w, ssd_conv_b, ssd_dt_bias, ssd_a_log, ssd_d, ssd_norm_g,
                   w_out, norm2_g, w_gate, w_up, w_down, final_norm_g)
    depth, bs = state_s5_re.shape[:2]
    past_len = 1024

    y_prompt, p_states = _trunk(x_prompt, None, prm, 0)
    s_in = (state_s5_re.reshape(depth, bs, SUBLANES, LANES), state_s5_im.reshape(depth, bs, SUBLANES, LANES),
            state_ret, state_ssd, cache_ssd_conv)
    y_sample, s_states = _trunk(x_sample, s_in, prm, past_len)
    return (y_prompt, y_sample) + p_states + s_states
```

```python
import functools
import math

import jax
import jax.numpy as jnp
from jax import lax
from jax.experimental import pallas as pl
from jax.experimental.pallas import tpu as pltpu

F32 = jnp.float32
BF16 = jnp.bfloat16

D_MODEL = 1024
EPS = 1e-6
ROPE_BASE = 10000.0
S5_WIDTH = 256
S5_GROUPS = 16
S5_GROUP_CH = 16
S5_STATE = 64
S5_LANES = S5_GROUPS * S5_STATE
RET_HEADS = 4
RET_DK = 64
RET_DV = 64
RET_WIDTH = RET_HEADS * RET_DV
SSD_WIDTH = 512
SSD_HEADDIM = 64
SSD_HEADS = 8
SSD_GROUPS = 2
SSD_DSTATE = 128
SSD_CONV = 4
SSD_CONV_DIM = SSD_WIDTH + 2 * SSD_GROUPS * SSD_DSTATE
D_FF = 2816
LANES = 128
SUBLANES = 8
MXU_DEPTH = 256

U0 = 0
Q0 = U0 + S5_WIDTH
K0 = Q0 + RET_HEADS * RET_DK
V0 = K0 + RET_HEADS * RET_DK
G0 = V0 + RET_WIDTH
Z0 = G0 + RET_WIDTH
X0 = Z0 + SSD_WIDTH
DT0 = X0 + SSD_CONV_DIM
IN_PAD = DT0 + LANES

CONV_TAIL = SSD_CONV - 1
CONV_LO = SUBLANES - CONV_TAIL

VMEM_LIMIT = 60000 * 1024


def _dot(a, b):
    return jnp.dot(a, b, preferred_element_type=F32)


def _dot_nt(a, b):
    return lax.dot_general(a, b, (((1,), (1,)), ((), ())), preferred_element_type=F32)


def _sigmoid(x):
    return 1.0 / (1.0 + jnp.exp(-x))


def _silu(x):
    return x * _sigmoid(x)


def _gelu_tanh(x):
    c = math.sqrt(2.0 / math.pi)
    return x * (0.5 * (1.0 + jnp.tanh(c * (x + 0.044715 * (x * x * x)))))


def _softplus(x):
    return jnp.maximum(x, 0.0) + jnp.log1p(jnp.exp(-jnp.abs(x)))


def _rms(x, g):
    ms = jnp.mean(x * x, axis=-1, keepdims=True)
    return x * lax.rsqrt(ms + EPS) * g


def _split3(x):
    hi = x.astype(BF16)
    r1 = x - hi.astype(F32)
    mid = r1.astype(BF16)
    lo = (r1 - mid.astype(F32)).astype(BF16)
    return jnp.concatenate([hi, mid, lo], axis=-1)


def _s5_param_kernel(lr_ref, li_ref, ldt_ref, br_ref, bi_ref, ar_ref, ai_ref, bbr_ref, bbi_ref):
    depth = lr_ref.shape[0]
    for l in range(depth):
        lr = lr_ref[l:l + 1, :]
        li = li_ref[l:l + 1, :]
        dt = jnp.exp(ldt_ref[l:l + 1, :])
        mag = jnp.exp(lr * dt)
        ar = mag * jnp.cos(li * dt)
        ai = mag * jnp.sin(li * dt)
        den = lr * lr + li * li
        nr = ar - 1.0
        ni = ai
        kr = (nr * lr + ni * li) / den
        ki = (ni * lr - nr * li) / den
        br = br_ref[l]
        bi = bi_ref[l]
        bbr_ref[l] = kr * br - ki * bi
        bbi_ref[l] = kr * bi + ki * br
        ar_ref[l:l + 1, :] = ar
        ai_ref[l:l + 1, :] = ai


def _s5_params(lam_re, lam_im, log_dt, b_re, b_im):
    depth = lam_re.shape[0]
    lr = lam_re.reshape(depth, S5_LANES)
    li = lam_im.reshape(depth, S5_LANES)
    ldt = jnp.repeat(log_dt, S5_STATE, axis=-1)
    brt = jnp.transpose(b_re, (0, 3, 1, 2)).reshape(depth, S5_GROUP_CH, S5_LANES)
    bit = jnp.transpose(b_im, (0, 3, 1, 2)).reshape(depth, S5_GROUP_CH, S5_LANES)
    out_shape = (jax.ShapeDtypeStruct((depth, S5_LANES), F32),
                 jax.ShapeDtypeStruct((depth, S5_LANES), F32),
                 jax.ShapeDtypeStruct((depth, S5_GROUP_CH, S5_LANES), F32),
                 jax.ShapeDtypeStruct((depth, S5_GROUP_CH, S5_LANES), F32))
    return pl.pallas_call(_s5_param_kernel, out_shape=out_shape, name="s5_params")(lr, li, ldt, brt, bit)


def _round_robin(chains, lead=None):
    chains = list(chains)
    first = True
    while chains:
        for c in list(chains):
            if next(c, StopIteration) is StopIteration:
                chains.remove(c)
        if first and lead is not None:
            for _ in lead:
                pass
        first = False


def _mixer_kernel(*refs, nb, tile, chunk, n_t, zero_init):
    (x_ref, g1_ref, win_ref, cos_ref, sin_ref,
     a8r_ref, a8i_ref, bb_ref, ct_ref, s5d_ref, wglu_ref, bglu_ref,
     dec_ref, qw_ref, kw_ref, cdc_ref, mbd_ref, mavg_ref, hmq_ref, hmv_ref, tri_ref, e3_ref,
     cw_ref, cb_ref, dtb_ref, alog_ref, dskip_ref, ng_ref, wout_ref) = refs[:29]
    refs = refs[29:]
    if not zero_init:
        s5r_in, s5i_in, ret_in, ssd_in, conv_in = refs[:5]
        refs = refs[5:]
    (xo_ref, s5r_out, s5i_out, ret_out, ssd_out, conv_out,
     proj_ref, hn_ref, mix_ref, hb_ref, xp_ref, ret_sc, ssd_sc, conv_sc) = refs

    C = chunk
    nchunk = tile // C
    ic = pl.program_id(0) % n_t if n_t > 1 else 0
    half = RET_HEADS * RET_DK // 2
    hpg = SSD_HEADS // SSD_GROUPS
    gw = hpg * SSD_HEADDIM
    ret_blocks = [(h, hf, hf * half + h * (RET_DK // 2), h * RET_DV)
                  for h in range(RET_HEADS) for hf in range(2)]

    def init_states():
        if zero_init:
            s5r_out[...] = jnp.zeros_like(s5r_out)
            s5i_out[...] = jnp.zeros_like(s5i_out)
            ret_sc[...] = jnp.zeros_like(ret_sc)
            ssd_sc[...] = jnp.zeros_like(ssd_sc)
            conv_sc[...] = jnp.zeros_like(conv_sc)
            return
        s5r_out[...] = s5r_in[...]
        s5i_out[...] = s5i_in[...]
        ret_sc[...] = jnp.zeros_like(ret_sc)
        conv_sc[...] = jnp.zeros_like(conv_sc)
        for b in range(nb):
            for h, hf, r0, c0 in ret_blocks:
                ret_sc[b, r0:r0 + RET_DK // 2, c0:c0 + RET_DV] = ret_in[b, h, hf * (RET_DK // 2):(hf + 1) * (RET_DK // 2), :]
            for g in range(SSD_GROUPS):
                ssd_sc[b, g] = ssd_in[b, g * hpg:(g + 1) * hpg].reshape(gw, SSD_DSTATE).T
            conv_sc[b, CONV_LO:SUBLANES, :] = conv_in[b]

    def final_states():
        for b in range(nb):
            for h, hf, r0, c0 in ret_blocks:
                ret_out[b, h, hf * (RET_DK // 2):(hf + 1) * (RET_DK // 2), :] = ret_sc[b, r0:r0 + RET_DK // 2, c0:c0 + RET_DV]
            for g in range(SSD_GROUPS):
                ssd_out[b, g * hpg:(g + 1) * hpg] = ssd_sc[b, g].T.reshape(hpg, SSD_HEADDIM, SSD_DSTATE)
            conv_out[b] = conv_sc[b, CONV_LO:SUBLANES, :]

    if n_t > 1:
        pl.when(ic == 0)(init_states)
    else:
        init_states()

    hn_ref[...] = _rms(x_ref[...], g1_ref[...]).astype(BF16)

    def proj_pieces(rows):
        hn = hn_ref[rows, :]
        for lo in range(0, IN_PAD, MXU_DEPTH):
            hi = min(lo + MXU_DEPTH, IN_PAD)
            proj_ref[rows, lo:hi] = _dot(hn, win_ref[:, lo:hi])
            yield

    causal = (lax.broadcasted_iota(jnp.int32, (C, C), 0) >= lax.broadcasted_iota(jnp.int32, (C, C), 1))
    lane128 = lax.broadcasted_iota(jnp.int32, (1, LANES), 1)

    def pslab(rows, lo, hi):
        return proj_ref[rows, lo:hi]

    def s5_chain(bi, ci, r0):
        rows = pl.ds(r0, C)
        u = pslab(rows,U0, U0 + S5_WIDTH)
        bu = _dot(u.astype(BF16), bb_ref[...])
        nslab = S5_LANES // LANES
        pitch = C + SUBLANES
        for s in range(2 * nslab):
            hb_ref[s * pitch:s * pitch + C, :] = bu[:, s * LANES:(s + 1) * LANES]
        yield
        ar = a8r_ref[...]
        ai = a8i_ref[...]
        hr = s5r_out[bi]
        hi = s5i_out[bi]
        for t in range(C):
            ld_r = pl.ds(t, nslab, stride=pitch)
            ld_i = pl.ds(nslab * pitch + t, nslab, stride=pitch)
            hr, hi = (ar * hr - ai * hi + hb_ref[ld_r, :], ar * hi + ai * hr + hb_ref[ld_i, :])
            hb_ref[ld_r, :] = hr
            hb_ref[ld_i, :] = hi
        s5r_out[bi] = hr
        s5i_out[bi] = hi
        yield
        hall = jnp.concatenate([hb_ref[s * pitch:s * pitch + C, :].astype(BF16) for s in range(2 * nslab)],
                               axis=-1)
        y = s5d_ref[...] * u + _dot_nt(hall, ct_ref[...])
        yield
        zg = _gelu_tanh(y)
        gl = _dot(zg.astype(BF16), wglu_ref[...]) + bglu_ref[...]
        yield
        mix_ref[rows, 0:S5_WIDTH] = zg * _sigmoid(gl)

    def ret_chain(bi, ci, r0):
        rows = pl.ds(r0, C)
        q1 = pslab(rows,Q0, Q0 + half)
        q2 = pslab(rows,Q0 + half, Q0 + 2 * half)
        k1 = pslab(rows,K0, K0 + half)
        k2 = pslab(rows,K0 + half, K0 + 2 * half)
        cs_ = cos_ref[pl.ds(ci * C, C), :]
        sn_ = sin_ref[pl.ds(ci * C, C), :]
        qr = jnp.concatenate([q1 * cs_ - q2 * sn_, q1 * sn_ + q2 * cs_], axis=-1)
        kr = jnp.concatenate([k1 * cs_ - k2 * sn_, k1 * sn_ + k2 * cs_], axis=-1) * (RET_DK ** -0.5)
        gate = pslab(rows,G0, G0 + RET_WIDTH)
        s_prev = ret_sc[bi]
        qb = qr.astype(BF16)
        kb = kr.astype(BF16)
        vb = pslab(rows,V0, V0 + RET_WIDTH).astype(BF16)
        o = _dot((qr * qw_ref[...]).astype(BF16), s_prev.astype(BF16))
        yield
        scs = []
        for h in range(RET_HEADS):
            scs.append(_dot_nt(qb * hmq_ref[h], kb) * dec_ref[h])
            yield
        kt = (kr * kw_ref[...]).T
        kv = _dot(kt.astype(BF16), vb)
        ret_sc[bi] = s_prev * cdc_ref[...] + kv * mbd_ref[...]
        yield
        for h in range(RET_HEADS):
            o = o + _dot(scs[h].astype(BF16), vb * hmv_ref[h])
            yield
        mavg = mavg_ref[...]
        mu = _dot(o.astype(BF16), mavg)
        yield
        dlt = o - mu
        var = _dot((dlt * dlt).astype(BF16), mavg)
        yield
        mix_ref[rows, S5_WIDTH:S5_WIDTH + RET_WIDTH] = _silu(gate) * (dlt * lax.rsqrt(var + EPS))

    def ssd_chain(bi, ci, r0):
        rows = pl.ds(r0, C)
        z = pslab(rows,Z0, Z0 + SSD_WIDTH)
        xbc = pslab(rows,X0, X0 + SSD_CONV_DIM)
        xp_ref[0:SUBLANES, :] = conv_sc[bi]
        xp_ref[SUBLANES:SUBLANES + C, :] = xbc
        cw = cw_ref[...]
        acc = cb_ref[...] + cw[CONV_TAIL:SSD_CONV] * xbc
        for i in range(CONV_TAIL):
            acc = acc + cw[i:i + 1] * xp_ref[CONV_LO + i:CONV_LO + i + C, :]
        conv_sc[bi] = xp_ref[C:C + SUBLANES, :]
        xc = _silu(acc)
        xs = xc[:, 0:SSD_WIDTH]
        ngl = SSD_DSTATE
        bm = xc[:, SSD_WIDTH:SSD_WIDTH + SSD_GROUPS * ngl]
        cm = xc[:, SSD_WIDTH + SSD_GROUPS * ngl:SSD_CONV_DIM]
        dt = _softplus(pslab(rows,DT0, DT0 + LANES) + dtb_ref[...])
        a_row = jnp.where(lane128 < SSD_HEADS, -jnp.exp(alog_ref[...]), 0.0)
        dta = dt * a_row
        cs3 = _dot(tri_ref[...], _split3(dta))
        yield
        cs = cs3[:, 0:LANES] + cs3[:, LANES:2 * LANES] + cs3[:, 2 * LANES:3 * LANES]
        dt_e = _dot(_split3(dt), e3_ref[...])
        yield
        cs_e = _dot(_split3(cs), e3_ref[...])
        yield
        last_e = cs_e[C - 1:C, :]
        ecs_e = jnp.exp(cs_e)
        wend_e = jnp.exp(last_e - cs_e)
        cdec_e = jnp.exp(last_e)
        xdt = xs * dt_e
        cst = cs.T
        ys = []
        for g in range(SSD_GROUPS):
            cm_g = cm[:, g * ngl:(g + 1) * ngl].astype(BF16)
            bm_g = bm[:, g * ngl:(g + 1) * ngl]
            cbm = _dot_nt(cm_g, bm_g.astype(BF16))
            yield
            st = ssd_sc[bi, g]
            xdt_g = xdt[:, g * gw:(g + 1) * gw]
            xdt_b = xdt_g.astype(BF16)
            yg = _dot(cm_g, st.astype(BF16)) * ecs_e[:, g * gw:(g + 1) * gw]
            yield
            new = _dot(bm_g.T.astype(BF16), (xdt_g * wend_e[:, g * gw:(g + 1) * gw]).astype(BF16))
            ssd_sc[bi, g] = st * cdec_e[:, g * gw:(g + 1) * gw] + new
            yield
            for hl in range(hpg):
                h = g * hpg + hl
                seg = cs[:, h:h + 1] - cst[h:h + 1, :]
                lm = jnp.where(causal, jnp.exp(seg), 0.0)
                yg = yg + _dot((cbm * lm).astype(BF16), xdt_b * hmv_ref[hl])
                yield
            ys.append(yg)
        yss = jnp.concatenate(ys, axis=-1) + dskip_ref[...] * xs
        mix_ref[rows, S5_WIDTH + RET_WIDTH:D_MODEL] = _rms(yss * _silu(z), ng_ref[...])

    def out_chain(rows):
        mixb = mix_ref[rows, :].astype(BF16)
        for lo in range(0, D_MODEL, D_MODEL // 2):
            cols = slice(lo, lo + D_MODEL // 2)
            xo_ref[rows, cols] = x_ref[rows, cols] + _dot(mixb, wout_ref[:, cols])
            yield

    def head_groups(bi, ci, r0):
        return [s5_chain(bi, ci, r0), ret_chain(bi, ci, r0), ssd_chain(bi, ci, r0)]

    if nb == 1:
        _round_robin([proj_pieces(pl.ds(0, C))])
        prev = []
        for ci in range(nchunk):
            ahead = proj_pieces(pl.ds((ci + 1) * C, C)) if ci + 1 < nchunk else None
            _round_robin(head_groups(0, ci, ci * C) + prev, ahead)
            prev = [out_chain(pl.ds(ci * C, C))]
        _round_robin(prev)
    else:
        _round_robin([proj_pieces(pl.ds(0, nb * C))])
        def seg_body(j, carry):
            _round_robin(head_groups(j, 0, pl.multiple_of(j * C, C)))
            return carry
        lax.fori_loop(0, nb, seg_body, 0)
        _round_robin([out_chain(pl.ds(0, nb * C))])

    if n_t > 1:
        pl.when(ic == n_t - 1)(final_states)
    else:
        final_states()


def _const_spec(shape, l=None):
    if l is None:
        return pl.BlockSpec(shape, lambda k: (0,) * len(shape), pipeline_mode=pl.Buffered(1))
    nd = len(shape)
    return pl.BlockSpec((None,) + tuple(shape), lambda k: (l,) + (0,) * nd, pipeline_mode=pl.Buffered(1))


def _mixer_call(x2d, states, prm, tabs, l, *, batch, seq, nb, tile, chunk):
    C = chunk
    n_t = seq // tile
    total = (batch // nb) * n_t
    rows = nb * tile
    zero_init = states is None
    row_spec = pl.BlockSpec((rows, D_MODEL), lambda k: (k, 0))
    rope_spec = pl.BlockSpec((tile, LANES), lambda k: (k % n_t, 0))
    st_shapes = [(SUBLANES, LANES), (SUBLANES, LANES), (RET_HEADS, RET_DK, RET_DV),
                 (SSD_HEADS, SSD_HEADDIM, SSD_DSTATE), (CONV_TAIL, SSD_CONV_DIM)]
    st_out_specs = [pl.BlockSpec((nb,) + s, lambda k, _n=len(s): (k // n_t,) + (0,) * _n) for s in st_shapes]
    st_in_specs = [pl.BlockSpec((None, nb) + s, lambda k, _n=len(s): (l, k // n_t) + (0,) * _n)
                   for s in st_shapes]
    in_specs = [
        row_spec,
        _const_spec((1, D_MODEL), l),
        _const_spec((D_MODEL, IN_PAD), l),
        rope_spec,
        rope_spec,
        _const_spec((SUBLANES, LANES), l), _const_spec((SUBLANES, LANES), l),
        _const_spec((S5_WIDTH, 2 * S5_LANES), l), _const_spec((S5_WIDTH, 2 * S5_LANES), l),
        _const_spec((1, S5_WIDTH), l), _const_spec((S5_WIDTH, S5_WIDTH), l), _const_spec((1, S5_WIDTH), l),
        _const_spec((RET_HEADS, C, C)), _const_spec((C, RET_WIDTH)), _const_spec((C, RET_WIDTH)),
        _const_spec((1, RET_WIDTH)), _const_spec((RET_WIDTH, RET_WIDTH)), _const_spec((RET_WIDTH, RET_WIDTH)),
        _const_spec((RET_HEADS, C, RET_WIDTH)), _const_spec((RET_HEADS, C, RET_WIDTH)),
        _const_spec((C, C)), _const_spec((3 * LANES, SSD_WIDTH)),
        _const_spec((SSD_CONV, SSD_CONV_DIM), l), _const_spec((1, SSD_CONV_DIM), l),
        _const_spec((1, LANES), l), _const_spec((1, LANES), l),
        _const_spec((1, SSD_WIDTH), l), _const_spec((1, SSD_WIDTH), l),
        _const_spec((D_MODEL, D_MODEL), l),
    ] + ([] if zero_init else st_in_specs)
    out_specs = [row_spec] + st_out_specs
    out_shape = ([jax.ShapeDtypeStruct(x2d.shape, F32)]
                 + [jax.ShapeDtypeStruct((batch,) + s, F32) for s in st_shapes])
    scratch = [pltpu.VMEM((rows, IN_PAD), F32), pltpu.VMEM((rows, D_MODEL), BF16),
               pltpu.VMEM((rows, D_MODEL), F32),
               pltpu.VMEM((2 * S5_LANES // LANES * (C + SUBLANES), LANES), F32),
               pltpu.VMEM((C + SUBLANES, SSD_CONV_DIM), F32),
               pltpu.VMEM((nb, RET_HEADS * RET_DK, RET_WIDTH), F32),
               pltpu.VMEM((nb, SSD_GROUPS, SSD_DSTATE, SSD_WIDTH // SSD_GROUPS), F32),
               pltpu.VMEM((nb, SUBLANES, SSD_CONV_DIM), F32)]
    kern = functools.partial(_mixer_kernel, nb=nb, tile=tile, chunk=C, n_t=n_t, zero_init=zero_init)
    outs = pl.pallas_call(
        kern, grid=(total,), in_specs=in_specs, out_specs=out_specs, out_shape=out_shape,
        scratch_shapes=scratch, name=f"mixer_c{C}",
        compiler_params=pltpu.CompilerParams(dimension_semantics=("arbitrary",),
                                             vmem_limit_bytes=VMEM_LIMIT),
    )(x2d, prm["g1"], prm["win"], tabs["cos"], tabs["sin"],
      prm["a8r"], prm["a8i"], prm["bb"], prm["ct"], prm["s5d"], prm["wglu"], prm["bglu"],
      tabs["dec"], tabs["qw"], tabs["kw"], tabs["cdc"], tabs["mbd"], tabs["mavg"], tabs["hmq"], tabs["hmv"],
      tabs["tri"], tabs["e3"],
      prm["cw"], prm["cb"], prm["dtb"], prm["alog"], prm["dskip"], prm["ng"], prm["wout"],
      *(() if zero_init else states))
    return outs[0], tuple(outs[1:])


def _ffn_kernel(x_ref, g2_ref, wg_ref, wu_ref, wd_ref, gf_ref, o_ref, *, final):
    x = x_ref[...]
    hn = _rms(x, g2_ref[...]).astype(BF16)
    acc = x
    split = (D_FF // (2 * MXU_DEPTH) + 1) * MXU_DEPTH
    for lo, hi in ((0, split), (split, D_FF)):
        gt = _dot(hn, wg_ref[:, lo:hi])
        up = _dot(hn, wu_ref[:, lo:hi])
        acc = acc + _dot((_silu(gt) * up).astype(BF16), wd_ref[lo:hi, :])
    if final:
        acc = _rms(acc, gf_ref[...])
    o_ref[...] = acc


def _ffn_call(x2d, prm, l, *, rows, final):
    n = x2d.shape[0] // rows
    row_spec = pl.BlockSpec((rows, D_MODEL), lambda i: (i, 0))

    def wspec(shape, layer=True):
        nd = len(shape)
        if layer:
            return pl.BlockSpec((None,) + shape, lambda i: (l,) + (0,) * nd, pipeline_mode=pl.Buffered(1))
        return pl.BlockSpec(shape, lambda i: (0,) * nd, pipeline_mode=pl.Buffered(1))

    return pl.pallas_call(
        functools.partial(_ffn_kernel, final=final), grid=(n,),
        in_specs=[row_spec, wspec((1, D_MODEL)), wspec((D_MODEL, D_FF)), wspec((D_MODEL, D_FF)),
                  wspec((D_FF, D_MODEL)), wspec((1, D_MODEL), layer=False)],
        out_specs=row_spec, out_shape=jax.ShapeDtypeStruct(x2d.shape, F32),
        name="ffn_final" if final else "ffn",
        compiler_params=pltpu.CompilerParams(dimension_semantics=("arbitrary",), vmem_limit_bytes=VMEM_LIMIT),
    )(x2d, prm["g2"], prm["wg"], prm["wu"], prm["wd"], prm["gf"])


def _reorder_w_in(w_in):
    depth = w_in.shape[0]
    qk = RET_HEADS * RET_DK

    def halves_first(w):
        w = w.reshape(depth, D_MODEL, RET_HEADS, 2, RET_DK // 2)
        return w.transpose(0, 1, 3, 2, 4).reshape(depth, D_MODEL, qk)

    wb = w_in.astype(BF16)
    parts = [wb[:, :, :S5_WIDTH], halves_first(wb[:, :, S5_WIDTH:S5_WIDTH + qk]),
             halves_first(wb[:, :, S5_WIDTH + qk:S5_WIDTH + 2 * qk]), wb[:, :, S5_WIDTH + 2 * qk:]]
    used = sum(p.shape[2] for p in parts)
    parts.append(jnp.zeros((depth, D_MODEL, IN_PAD - used), BF16))
    return jnp.concatenate(parts, axis=2)


def _prepare(norm1_g, w_in, s5_lam_re, s5_lam_im, s5_log_dt, s5_b_re, s5_b_im, s5_c_re, s5_c_im,
             s5_d, s5_w_glu, s5_b_glu, ssd_conv_w, ssd_conv_b, ssd_dt_bias, ssd_a_log, ssd_d, ssd_norm_g,
             w_out, norm2_g, w_gate, w_up, w_down, final_norm_g):
    depth = w_in.shape[0]
    a_re, a_im, bbr, bbi = _s5_params(s5_lam_re, s5_lam_im, s5_log_dt, s5_b_re, s5_b_im)
    rg = jnp.arange(S5_WIDTH)[:, None] // S5_GROUP_CH
    cg = jnp.arange(S5_LANES)[None, :] // S5_STATE
    blk = (rg == cg)[None]

    def bdiag_b(t):
        return jnp.where(blk, jnp.tile(t, (1, S5_GROUPS, 1)), 0.0)

    def bdiag_c(c):
        c2 = c.reshape(depth, S5_WIDTH, S5_STATE)
        return jnp.where(blk, jnp.tile(c2, (1, 1, S5_GROUPS)), 0.0)

    pad_l = ((0, 0), (0, 0), (0, LANES - SSD_HEADS))
    return {
        "g1": norm1_g[:, None, :], "win": _reorder_w_in(w_in),
        "a8r": a_re.reshape(depth, SUBLANES, LANES), "a8i": a_im.reshape(depth, SUBLANES, LANES),
        "bb": jnp.concatenate([bdiag_b(bbr), bdiag_b(bbi)], axis=-1).astype(BF16),
        "ct": jnp.concatenate([bdiag_c(s5_c_re), -bdiag_c(s5_c_im)], axis=-1).astype(BF16),
        "s5d": s5_d[:, None, :], "wglu": s5_w_glu.astype(BF16), "bglu": s5_b_glu[:, None, :],
        "cw": ssd_conv_w, "cb": ssd_conv_b[:, None, :],
        "dtb": jnp.pad(ssd_dt_bias[:, None, :], pad_l), "alog": jnp.pad(ssd_a_log[:, None, :], pad_l),
        "dskip": jnp.repeat(ssd_d, SSD_HEADDIM, axis=-1)[:, None, :], "ng": ssd_norm_g[:, None, :],
        "wout": w_out.astype(BF16), "g2": norm2_g[:, None, :],
        "wg": w_gate.astype(BF16), "wu": w_up.astype(BF16), "wd": w_down.astype(BF16),
        "gf": final_norm_g[None, :],
    }


def _tables(seq, chunk, t0):
    C = chunk
    lg = [math.log(1.0 - 2.0 ** (-5.0 - h)) for h in range(RET_HEADS)]
    lgv = jnp.asarray(lg, F32)
    i = jnp.arange(C, dtype=F32)
    rel = i[:, None] - i[None, :]
    dec = jnp.where(rel >= 0, jnp.exp(lgv[:, None, None] * jnp.maximum(rel, 0.0)), 0.0)
    lane = jnp.arange(RET_WIDTH)
    hq = (lane % (RET_WIDTH // 2)) // (RET_DK // 2)
    hv = lane // RET_DV
    qw = jnp.exp(lgv[hq][None, :] * (i + 1.0)[:, None])
    kw = jnp.exp(lgv[hq][None, :] * (C - 1.0 - i)[:, None])
    cdc = jnp.exp(lgv[hv] * C)[None, :]
    mbd = (hq[:, None] == hv[None, :]).astype(F32)
    mavg = jnp.where(hv[:, None] == hv[None, :], 1.0 / RET_DV, 0.0).astype(BF16)
    heads = jnp.arange(RET_HEADS)[:, None, None]
    hmq = jnp.broadcast_to(hq[None, None, :] == heads, (RET_HEADS, C, RET_WIDTH)).astype(BF16)
    hmv = jnp.broadcast_to(hv[None, None, :] == heads, (RET_HEADS, C, RET_WIDTH)).astype(BF16)
    tri = (i[:, None] >= i[None, :]).astype(BF16)
    r = jnp.arange(3 * LANES) % LANES
    e3 = (r[:, None] == (jnp.arange(SSD_WIDTH) // SSD_HEADDIM)[None, :]).astype(BF16)
    half = RET_DK // 2
    inv_freq = ROPE_BASE ** (-jnp.arange(half, dtype=F32) / half)
    pos = (t0 + jnp.arange(seq)).astype(F32)
    ang = pos[:, None] * jnp.tile(inv_freq, RET_HEADS)[None, :]
    return {"dec": dec, "qw": qw, "kw": kw, "cdc": cdc, "mbd": mbd, "mavg": mavg, "hmq": hmq, "hmv": hmv,
            "tri": tri, "e3": e3, "cos": jnp.cos(ang), "sin": jnp.sin(ang)}


def _plan(batch, seq):
    if seq >= 512:
        return 1, 512, 256
    nb = max(1, min(batch, 256 // seq))
    while batch % nb:
        nb -= 1
    return nb, seq, seq


def _trunk(x, states, prm, t0):
    batch, seq, _ = x.shape
    nb, tile, chunk = _plan(batch, seq)
    tabs = _tables(seq, chunk, t0)
    depth = prm["win"].shape[0]
    x2d = x.reshape(batch * seq, D_MODEL)
    ffn_rows = min(512, batch * seq)
    outs = []
    for l in range(depth):
        x2d, st_out = _mixer_call(x2d, states, prm, tabs, l, batch=batch, seq=seq, nb=nb, tile=tile, chunk=chunk)
        x2d = _ffn_call(x2d, prm, l, rows=ffn_rows, final=(l == depth - 1))
        outs.append(st_out)
    stacked = [jnp.stack([o[i] for o in outs]) for i in range(5)]
    s5_shape = (depth, batch, S5_GROUPS, S5_STATE)
    return (x2d.reshape(batch, seq, D_MODEL),
            (stacked[0].reshape(s5_shape), stacked[1].reshape(s5_shape), stacked[2], stacked[3], stacked[4]))


def kernel(x_prompt, x_sample, state_s5_re, state_s5_im, state_ret, state_ssd, cache_ssd_conv, norm1_g, w_in, s5_lam_re, s5_lam_im, s5_log_dt, s5_b_re, s5_b_im, s5_c_re, s5_c_im, s5_d, s5_w_glu, s5_b_glu, ssd_conv_w, ssd_conv_b, ssd_dt_bias, ssd_a_log, ssd_d, ssd_norm_g, w_out, norm2_g, w_gate, w_up, w_down, final_norm_g):
    prm = _prepare(norm1_g, w_in, s5_lam_re, s5_lam_im, s5_log_dt, s5_b_re, s5_b_im, s5_c_re, s5_c_im,
                   s5_d, s5_w_glu, s5_b_glu, ssd_conv_w, ssd_conv_b, ssd_dt_bias, ssd_a_log, ssd_d, ssd_norm_g,
                   w_out, norm2_g, w_gate, w_up, w_down, final_norm_g)
    depth, bs = state_s5_re.shape[:2]
    past_len = 1024

    y_prompt, p_states = _trunk(x_prompt, None, prm, 0)
    s_in = (state_s5_re.reshape(depth, bs, SUBLANES, LANES), state_s5_im.reshape(depth, bs, SUBLANES, LANES),
            state_ret, state_ssd, cache_ssd_conv)
    y_sample, s_states = _trunk(x_sample, s_in, prm, past_len)
    return (y_prompt, y_sample) + p_states + s_states
```

```python
import functools
import math

import jax
import jax.numpy as jnp
from jax import lax
from jax.experimental import pallas as pl
from jax.experimental.pallas import tpu as pltpu

F32 = jnp.float32
BF16 = jnp.bfloat16

D_MODEL = 1024
EPS = 1e-6
ROPE_BASE = 10000.0
S5_WIDTH = 256
S5_GROUPS = 16
S5_GROUP_CH = 16
S5_STATE = 64
S5_LANES = S5_GROUPS * S5_STATE
RET_HEADS = 4
RET_DK = 64
RET_DV = 64
RET_WIDTH = RET_HEADS * RET_DV
SSD_WIDTH = 512
SSD_HEADDIM = 64
SSD_HEADS = 8
SSD_GROUPS = 2
SSD_DSTATE = 128
SSD_CONV = 4
SSD_CONV_DIM = SSD_WIDTH + 2 * SSD_GROUPS * SSD_DSTATE
D_FF = 2816
LANES = 128
SUBLANES = 8
MXU_DEPTH = 256

U0 = 0
Q0 = U0 + S5_WIDTH
K0 = Q0 + RET_HEADS * RET_DK
V0 = K0 + RET_HEADS * RET_DK
G0 = V0 + RET_WIDTH
Z0 = G0 + RET_WIDTH
X0 = Z0 + SSD_WIDTH
DT0 = X0 + SSD_CONV_DIM
IN_PAD = DT0 + LANES

CONV_TAIL = SSD_CONV - 1
CONV_LO = SUBLANES - CONV_TAIL

VMEM_LIMIT = 60000 * 1024


def _dot(a, b):
    return jnp.dot(a, b, preferred_element_type=F32)


def _dot_nt(a, b):
    return lax.dot_general(a, b, (((1,), (1,)), ((), ())), preferred_element_type=F32)


def _sigmoid(x):
    return 1.0 / (1.0 + jnp.exp(-x))


def _silu(x):
    return x * _sigmoid(x)


def _gelu_tanh(x):
    c = math.sqrt(2.0 / math.pi)
    return x * (0.5 * (1.0 + jnp.tanh(c * (x + 0.044715 * (x * x * x)))))


def _softplus(x):
    return jnp.maximum(x, 0.0) + jnp.log1p(jnp.exp(-jnp.abs(x)))


def _rms(x, g):
    ms = jnp.mean(x * x, axis=-1, keepdims=True)
    return x * lax.rsqrt(ms + EPS) * g


def _split3(x):
    hi = x.astype(BF16)
    r1 = x - hi.astype(F32)
    mid = r1.astype(BF16)
    lo = (r1 - mid.astype(F32)).astype(BF16)
    return jnp.concatenate([hi, mid, lo], axis=-1)


def _s5_param_kernel(lr_ref, li_ref, ldt_ref, br_ref, bi_ref, ar_ref, ai_ref, bbr_ref, bbi_ref):
    depth = lr_ref.shape[0]
    for l in range(depth):
        lr = lr_ref[l:l + 1, :]
        li = li_ref[l:l + 1, :]
        dt = jnp.exp(ldt_ref[l:l + 1, :])
        mag = jnp.exp(lr * dt)
        ar = mag * jnp.cos(li * dt)
        ai = mag * jnp.sin(li * dt)
        den = lr * lr + li * li
        nr = ar - 1.0
        ni = ai
        kr = (nr * lr + ni * li) / den
        ki = (ni * lr - nr * li) / den
        br = br_ref[l]
        bi = bi_ref[l]
        bbr_ref[l] = kr * br - ki * bi
        bbi_ref[l] = kr * bi + ki * br
        ar_ref[l:l + 1, :] = ar
        ai_ref[l:l + 1, :] = ai


def _s5_params(lam_re, lam_im, log_dt, b_re, b_im):
    depth = lam_re.shape[0]
    lr = lam_re.reshape(depth, S5_LANES)
    li = lam_im.reshape(depth, S5_LANES)
    ldt = jnp.repeat(log_dt, S5_STATE, axis=-1)
    brt = jnp.transpose(b_re, (0, 3, 1, 2)).reshape(depth, S5_GROUP_CH, S5_LANES)
    bit = jnp.transpose(b_im, (0, 3, 1, 2)).reshape(depth, S5_GROUP_CH, S5_LANES)
    out_shape = (jax.ShapeDtypeStruct((depth, S5_LANES), F32),
                 jax.ShapeDtypeStruct((depth, S5_LANES), F32),
                 jax.ShapeDtypeStruct((depth, S5_GROUP_CH, S5_LANES), F32),
                 jax.ShapeDtypeStruct((depth, S5_GROUP_CH, S5_LANES), F32))
    return pl.pallas_call(_s5_param_kernel, out_shape=out_shape, name="s5_params")(lr, li, ldt, brt, bit)


def _round_robin(chains, lead=None):
    chains = list(chains)
    first = True
    while chains:
        for c in list(chains):
            if next(c, StopIteration) is StopIteration:
                chains.remove(c)
        if first and lead is not None:
            for _ in lead:
                pass
        first = False


def _mixer_kernel(*refs, nb, tile, chunk, n_t, zero_init):
    (x_ref, g1_ref, win_ref, cos_ref, sin_ref,
     a8r_ref, a8i_ref, bb_ref, ct_ref, s5d_ref, wglu_ref, bglu_ref,
     dec_ref, qw_ref, kw_ref, cdc_ref, mbd_ref, mavg_ref, hmq_ref, hmv_ref, tri_ref, e3_ref,
     cw_ref, cb_ref, dtb_ref, alog_ref, dskip_ref, ng_ref, wout_ref) = refs[:29]
    refs = refs[29:]
    if not zero_init:
        s5r_in, s5i_in, ret_in, ssd_in, conv_in = refs[:5]
        refs = refs[5:]
    (xo_ref, s5r_out, s5i_out, ret_out, ssd_out, conv_out,
     proj_ref, hn_ref, mix_ref, hb_all, xp_all, ret_sc, ssd_sc, conv_sc) = refs
    unrolled = tile > chunk

    C = chunk
    nchunk = tile // C
    ic = pl.program_id(0) % n_t if n_t > 1 else 0
    half = RET_HEADS * RET_DK // 2
    hpg = SSD_HEADS // SSD_GROUPS
    gw = hpg * SSD_HEADDIM
    ret_blocks = [(h, hf, hf * half + h * (RET_DK // 2), h * RET_DV)
                  for h in range(RET_HEADS) for hf in range(2)]

    def init_states():
        if zero_init:
            s5r_out[...] = jnp.zeros_like(s5r_out)
            s5i_out[...] = jnp.zeros_like(s5i_out)
            ret_sc[...] = jnp.zeros_like(ret_sc)
            ssd_sc[...] = jnp.zeros_like(ssd_sc)
            conv_sc[...] = jnp.zeros_like(conv_sc)
            return
        s5r_out[...] = s5r_in[...]
        s5i_out[...] = s5i_in[...]
        ret_sc[...] = jnp.zeros_like(ret_sc)
        conv_sc[...] = jnp.zeros_like(conv_sc)
        for b in range(nb):
            for h, hf, r0, c0 in ret_blocks:
                ret_sc[b, r0:r0 + RET_DK // 2, c0:c0 + RET_DV] = ret_in[b, h, hf * (RET_DK // 2):(hf + 1) * (RET_DK // 2), :]
            for g in range(SSD_GROUPS):
                ssd_sc[b, g] = ssd_in[b, g * hpg:(g + 1) * hpg].reshape(gw, SSD_DSTATE).T
            conv_sc[b, CONV_LO:SUBLANES, :] = conv_in[b]

    def final_states():
        for b in range(nb):
            for h, hf, r0, c0 in ret_blocks:
                ret_out[b, h, hf * (RET_DK // 2):(hf + 1) * (RET_DK // 2), :] = ret_sc[b, r0:r0 + RET_DK // 2, c0:c0 + RET_DV]
            for g in range(SSD_GROUPS):
                ssd_out[b, g * hpg:(g + 1) * hpg] = ssd_sc[b, g].T.reshape(hpg, SSD_HEADDIM, SSD_DSTATE)
            conv_out[b] = conv_sc[b, CONV_LO:SUBLANES, :]

    if n_t > 1:
        pl.when(ic == 0)(init_states)
    else:
        init_states()

    hn_ref[...] = _rms(x_ref[...].reshape(nb * tile, D_MODEL), g1_ref[...]).astype(BF16)

    def proj_pieces(rows):
        hn = hn_ref[rows, :]
        for lo in range(0, IN_PAD, MXU_DEPTH):
            hi = min(lo + MXU_DEPTH, IN_PAD)
            proj_ref[rows, lo:hi] = _dot(hn, win_ref[:, lo:hi])
            yield

    causal = (lax.broadcasted_iota(jnp.int32, (C, C), 0) >= lax.broadcasted_iota(jnp.int32, (C, C), 1))
    lane128 = lax.broadcasted_iota(jnp.int32, (1, LANES), 1)

    def pslab(rows, lo, hi):
        return proj_ref[rows, lo:hi]

    def s5_chain(bi, ci, r0):
        rows = pl.ds(r0, C)
        hb_ref = hb_all.at[bi if unrolled else 0]
        u = pslab(rows,U0, U0 + S5_WIDTH)
        bu = _dot(u.astype(BF16), bb_ref[...])
        nslab = S5_LANES // LANES
        pitch = C + SUBLANES
        for s in range(2 * nslab):
            hb_ref[s * pitch:s * pitch + C, :] = bu[:, s * LANES:(s + 1) * LANES]
        yield
        ar = a8r_ref[...]
        ai = a8i_ref[...]
        hr = s5r_out[bi]
        hi = s5i_out[bi]
        for t in range(C):
            ld_r = pl.ds(t, nslab, stride=pitch)
            ld_i = pl.ds(nslab * pitch + t, nslab, stride=pitch)
            hr, hi = (ar * hr - ai * hi + hb_ref[ld_r, :], ar * hi + ai * hr + hb_ref[ld_i, :])
            hb_ref[ld_r, :] = hr
            hb_ref[ld_i, :] = hi
        s5r_out[bi] = hr
        s5i_out[bi] = hi
        yield
        hall = jnp.concatenate([hb_ref[s * pitch:s * pitch + C, :].astype(BF16) for s in range(2 * nslab)],
                               axis=-1)
        y = s5d_ref[...] * u + _dot_nt(hall, ct_ref[...])
        yield
        zg = _gelu_tanh(y)
        gl = _dot(zg.astype(BF16), wglu_ref[...]) + bglu_ref[...]
        yield
        mix_ref[rows, 0:S5_WIDTH] = zg * _sigmoid(gl)

    def ret_chain(bi, ci, r0):
        rows = pl.ds(r0, C)
        q1 = pslab(rows,Q0, Q0 + half)
        q2 = pslab(rows,Q0 + half, Q0 + 2 * half)
        k1 = pslab(rows,K0, K0 + half)
        k2 = pslab(rows,K0 + half, K0 + 2 * half)
        cs_ = cos_ref[pl.ds(ci * C, C), :]
        sn_ = sin_ref[pl.ds(ci * C, C), :]
        qr = jnp.concatenate([q1 * cs_ - q2 * sn_, q1 * sn_ + q2 * cs_], axis=-1)
        kr = jnp.concatenate([k1 * cs_ - k2 * sn_, k1 * sn_ + k2 * cs_], axis=-1) * (RET_DK ** -0.5)
        gate = pslab(rows,G0, G0 + RET_WIDTH)
        s_prev = ret_sc[bi]
        qb = qr.astype(BF16)
        kb = kr.astype(BF16)
        vb = pslab(rows,V0, V0 + RET_WIDTH).astype(BF16)
        o = _dot((qr * qw_ref[...]).astype(BF16), s_prev.astype(BF16))
        yield
        scs = []
        for h in range(RET_HEADS):
            scs.append(_dot_nt(qb * hmq_ref[h], kb) * dec_ref[h])
            yield
        kt = (kr * kw_ref[...]).T
        kv = _dot(kt.astype(BF16), vb)
        ret_sc[bi] = s_prev * cdc_ref[...] + kv * mbd_ref[...]
        yield
        for h in range(RET_HEADS):
            o = o + _dot(scs[h].astype(BF16), vb * hmv_ref[h])
            yield
        mavg = mavg_ref[...]
        mu = _dot(o.astype(BF16), mavg)
        yield
        dlt = o - mu
        var = _dot((dlt * dlt).astype(BF16), mavg)
        yield
        mix_ref[rows, S5_WIDTH:S5_WIDTH + RET_WIDTH] = _silu(gate) * (dlt * lax.rsqrt(var + EPS))

    def ssd_chain(bi, ci, r0):
        rows = pl.ds(r0, C)
        z = pslab(rows,Z0, Z0 + SSD_WIDTH)
        xbc = pslab(rows,X0, X0 + SSD_CONV_DIM)
        xp_ref = xp_all.at[bi if unrolled else 0]
        xp_ref[0:SUBLANES, :] = conv_sc[bi]
        xp_ref[SUBLANES:SUBLANES + C, :] = xbc
        cw = cw_ref[...]
        acc = cb_ref[...] + cw[CONV_TAIL:SSD_CONV] * xbc
        for i in range(CONV_TAIL):
            acc = acc + cw[i:i + 1] * xp_ref[CONV_LO + i:CONV_LO + i + C, :]
        conv_sc[bi] = xp_ref[C:C + SUBLANES, :]
        xc = _silu(acc)
        xs = xc[:, 0:SSD_WIDTH]
        ngl = SSD_DSTATE
        bm = xc[:, SSD_WIDTH:SSD_WIDTH + SSD_GROUPS * ngl]
        cm = xc[:, SSD_WIDTH + SSD_GROUPS * ngl:SSD_CONV_DIM]
        dt = _softplus(pslab(rows,DT0, DT0 + LANES) + dtb_ref[...])
        a_row = jnp.where(lane128 < SSD_HEADS, -jnp.exp(alog_ref[...]), 0.0)
        dta = dt * a_row
        cs3 = _dot(tri_ref[...], _split3(dta))
        yield
        cs = cs3[:, 0:LANES] + cs3[:, LANES:2 * LANES] + cs3[:, 2 * LANES:3 * LANES]
        dt_e = _dot(_split3(dt), e3_ref[...])
        yield
        cs_e = _dot(_split3(cs), e3_ref[...])
        yield
        last_e = cs_e[C - 1:C, :]
        ecs_e = jnp.exp(cs_e)
        wend_e = jnp.exp(last_e - cs_e)
        cdec_e = jnp.exp(last_e)
        xdt = xs * dt_e
        cst = cs.T
        ys = []
        for g in range(SSD_GROUPS):
            cm_g = cm[:, g * ngl:(g + 1) * ngl].astype(BF16)
            bm_g = bm[:, g * ngl:(g + 1) * ngl]
            cbm = _dot_nt(cm_g, bm_g.astype(BF16))
            yield
            st = ssd_sc[bi, g]
            xdt_g = xdt[:, g * gw:(g + 1) * gw]
            xdt_b = xdt_g.astype(BF16)
            yg = _dot(cm_g, st.astype(BF16)) * ecs_e[:, g * gw:(g + 1) * gw]
            yield
            new = _dot(bm_g.T.astype(BF16), (xdt_g * wend_e[:, g * gw:(g + 1) * gw]).astype(BF16))
            ssd_sc[bi, g] = st * cdec_e[:, g * gw:(g + 1) * gw] + new
            yield
            for hl in range(hpg):
                h = g * hpg + hl
                seg = cs[:, h:h + 1] - cst[h:h + 1, :]
                lm = jnp.where(causal, jnp.exp(seg), 0.0)
                yg = yg + _dot((cbm * lm).astype(BF16), xdt_b * hmv_ref[hl])
                yield
            ys.append(yg)
        yss = jnp.concatenate(ys, axis=-1) + dskip_ref[...] * xs
        mix_ref[rows, S5_WIDTH + RET_WIDTH:D_MODEL] = _rms(yss * _silu(z), ng_ref[...])

    def out_chain(b, t0, n):
        mixb = mix_ref[b * tile + t0:b * tile + t0 + n, :].astype(BF16)
        for lo in range(0, D_MODEL, D_MODEL // 2):
            cols = slice(lo, lo + D_MODEL // 2)
            xo_ref[b, t0:t0 + n, cols] = x_ref[b, t0:t0 + n, cols] + _dot(mixb, wout_ref[:, cols])
            yield

    def head_groups(bi, ci, r0):
        return [s5_chain(bi, ci, r0), ret_chain(bi, ci, r0), ssd_chain(bi, ci, r0)]

    def chain_all(gens):
        for g in gens:
            yield from g

    if unrolled:
        _round_robin([proj_pieces(pl.ds(b * tile, C)) for b in range(nb)])
        prev = []
        for ci in range(nchunk):
            ahead = None
            if ci + 1 < nchunk:
                ahead = chain_all([proj_pieces(pl.ds(b * tile + (ci + 1) * C, C)) for b in range(nb)])
            groups = [g for b in range(nb) for g in head_groups(b, ci, b * tile + ci * C)]
            _round_robin(groups + prev, ahead)
            prev = [out_chain(b, ci * C, C) for b in range(nb)]
        _round_robin(prev)
    else:
        proj_ref[...] = _dot(hn_ref[...], win_ref[...])

        def seg_body(j, carry):
            _round_robin(head_groups(j, 0, pl.multiple_of(j * C, C)))
            return carry
        lax.fori_loop(0, nb, seg_body, 0)
        res = x_ref[...].reshape(nb * tile, D_MODEL) + _dot(mix_ref[...].astype(BF16), wout_ref[...])
        xo_ref[...] = res.reshape(nb, tile, D_MODEL)

    if n_t > 1:
        pl.when(ic == n_t - 1)(final_states)
    else:
        final_states()


def _const_spec(shape, l=None):
    if l is None:
        return pl.BlockSpec(shape, lambda k: (0,) * len(shape), pipeline_mode=pl.Buffered(1))
    nd = len(shape)
    return pl.BlockSpec((None,) + tuple(shape), lambda k: (l,) + (0,) * nd, pipeline_mode=pl.Buffered(1))


def _mixer_call(x2d, states, prm, tabs, l, *, batch, seq, nb, tile, chunk):
    C = chunk
    n_t = seq // tile
    total = (batch // nb) * n_t
    rows = nb * tile
    zero_init = states is None
    x3d = x2d.reshape(batch, seq, D_MODEL)
    row_spec = pl.BlockSpec((nb, tile, D_MODEL), lambda k: (k // n_t, k % n_t, 0))
    ways = nb if tile > C else 1
    rope_spec = pl.BlockSpec((tile, LANES), lambda k: (k % n_t, 0))
    st_shapes = [(SUBLANES, LANES), (SUBLANES, LANES), (RET_HEADS, RET_DK, RET_DV),
                 (SSD_HEADS, SSD_HEADDIM, SSD_DSTATE), (CONV_TAIL, SSD_CONV_DIM)]
    st_out_specs = [pl.BlockSpec((nb,) + s, lambda k, _n=len(s): (k // n_t,) + (0,) * _n) for s in st_shapes]
    st_in_specs = [pl.BlockSpec((None, nb) + s, lambda k, _n=len(s): (l, k // n_t) + (0,) * _n)
                   for s in st_shapes]
    in_specs = [
        row_spec,
        _const_spec((1, D_MODEL), l),
        _const_spec((D_MODEL, IN_PAD), l),
        rope_spec,
        rope_spec,
        _const_spec((SUBLANES, LANES), l), _const_spec((SUBLANES, LANES), l),
        _const_spec((S5_WIDTH, 2 * S5_LANES), l), _const_spec((S5_WIDTH, 2 * S5_LANES), l),
        _const_spec((1, S5_WIDTH), l), _const_spec((S5_WIDTH, S5_WIDTH), l), _const_spec((1, S5_WIDTH), l),
        _const_spec((RET_HEADS, C, C)), _const_spec((C, RET_WIDTH)), _const_spec((C, RET_WIDTH)),
        _const_spec((1, RET_WIDTH)), _const_spec((RET_WIDTH, RET_WIDTH)), _const_spec((RET_WIDTH, RET_WIDTH)),
        _const_spec((RET_HEADS, C, RET_WIDTH)), _const_spec((RET_HEADS, C, RET_WIDTH)),
        _const_spec((C, C)), _const_spec((3 * LANES, SSD_WIDTH)),
        _const_spec((SSD_CONV, SSD_CONV_DIM), l), _const_spec((1, SSD_CONV_DIM), l),
        _const_spec((1, LANES), l), _const_spec((1, LANES), l),
        _const_spec((1, SSD_WIDTH), l), _const_spec((1, SSD_WIDTH), l),
        _const_spec((D_MODEL, D_MODEL), l),
    ] + ([] if zero_init else st_in_specs)
    out_specs = [row_spec] + st_out_specs
    out_shape = ([jax.ShapeDtypeStruct(x3d.shape, F32)]
                 + [jax.ShapeDtypeStruct((batch,) + s, F32) for s in st_shapes])
    scratch = [pltpu.VMEM((rows, IN_PAD), F32), pltpu.VMEM((rows, D_MODEL), BF16),
               pltpu.VMEM((rows, D_MODEL), F32),
               pltpu.VMEM((ways, 2 * S5_LANES // LANES * (C + SUBLANES), LANES), F32),
               pltpu.VMEM((ways, C + SUBLANES, SSD_CONV_DIM), F32),
               pltpu.VMEM((nb, RET_HEADS * RET_DK, RET_WIDTH), F32),
               pltpu.VMEM((nb, SSD_GROUPS, SSD_DSTATE, SSD_WIDTH // SSD_GROUPS), F32),
               pltpu.VMEM((nb, SUBLANES, SSD_CONV_DIM), F32)]
    kern = functools.partial(_mixer_kernel, nb=nb, tile=tile, chunk=C, n_t=n_t, zero_init=zero_init)
    outs = pl.pallas_call(
        kern, grid=(total,), in_specs=in_specs, out_specs=out_specs, out_shape=out_shape,
        scratch_shapes=scratch, name=f"mixer_c{C}",
        compiler_params=pltpu.CompilerParams(dimension_semantics=("arbitrary",),
                                             vmem_limit_bytes=VMEM_LIMIT),
    )(x3d, prm["g1"], prm["win"], tabs["cos"], tabs["sin"],
      prm["a8r"], prm["a8i"], prm["bb"], prm["ct"], prm["s5d"], prm["wglu"], prm["bglu"],
      tabs["dec"], tabs["qw"], tabs["kw"], tabs["cdc"], tabs["mbd"], tabs["mavg"], tabs["hmq"], tabs["hmv"],
      tabs["tri"], tabs["e3"],
      prm["cw"], prm["cb"], prm["dtb"], prm["alog"], prm["dskip"], prm["ng"], prm["wout"],
      *(() if zero_init else states))
    return outs[0].reshape(x2d.shape), tuple(outs[1:])


def _ffn_kernel(x_ref, g2_ref, wg_ref, wu_ref, wd_ref, gf_ref, o_ref, *, final):
    x = x_ref[...]
    hn = _rms(x, g2_ref[...]).astype(BF16)
    acc = x
    split = (D_FF // (2 * MXU_DEPTH) + 1) * MXU_DEPTH
    for lo, hi in ((0, split), (split, D_FF)):
        gt = _dot(hn, wg_ref[:, lo:hi])
        up = _dot(hn, wu_ref[:, lo:hi])
        acc = acc + _dot((_silu(gt) * up).astype(BF16), wd_ref[lo:hi, :])
    if final:
        acc = _rms(acc, gf_ref[...])
    o_ref[...] = acc


def _ffn_call(x2d, prm, l, *, rows, final):
    n = x2d.shape[0] // rows
    row_spec = pl.BlockSpec((rows, D_MODEL), lambda i: (i, 0))

    def wspec(shape, layer=True):
        nd = len(shape)
        if layer:
            return pl.BlockSpec((None,) + shape, lambda i: (l,) + (0,) * nd, pipeline_mode=pl.Buffered(1))
        return pl.BlockSpec(shape, lambda i: (0,) * nd, pipeline_mode=pl.Buffered(1))

    return pl.pallas_call(
        functools.partial(_ffn_kernel, final=final), grid=(n,),
        in_specs=[row_spec, wspec((1, D_MODEL)), wspec((D_MODEL, D_FF)), wspec((D_MODEL, D_FF)),
                  wspec((D_FF, D_MODEL)), wspec((1, D_MODEL), layer=False)],
        out_specs=row_spec, out_shape=jax.ShapeDtypeStruct(x2d.shape, F32),
        name="ffn_final" if final else "ffn",
        compiler_params=pltpu.CompilerParams(dimension_semantics=("arbitrary",), vmem_limit_bytes=VMEM_LIMIT),
    )(x2d, prm["g2"], prm["wg"], prm["wu"], prm["wd"], prm["gf"])


def _reorder_w_in(w_in):
    depth = w_in.shape[0]
    qk = RET_HEADS * RET_DK

    def halves_first(w):
        w = w.reshape(depth, D_MODEL, RET_HEADS, 2, RET_DK // 2)
        return w.transpose(0, 1, 3, 2, 4).reshape(depth, D_MODEL, qk)

    wb = w_in.astype(BF16)
    parts = [wb[:, :, :S5_WIDTH], halves_first(wb[:, :, S5_WIDTH:S5_WIDTH + qk]),
             halves_first(wb[:, :, S5_WIDTH + qk:S5_WIDTH + 2 * qk]), wb[:, :, S5_WIDTH + 2 * qk:]]
    used = sum(p.shape[2] for p in parts)
    parts.append(jnp.zeros((depth, D_MODEL, IN_PAD - used), BF16))
    return jnp.concatenate(parts, axis=2)


def _prepare(norm1_g, w_in, s5_lam_re, s5_lam_im, s5_log_dt, s5_b_re, s5_b_im, s5_c_re, s5_c_im,
             s5_d, s5_w_glu, s5_b_glu, ssd_conv_w, ssd_conv_b, ssd_dt_bias, ssd_a_log, ssd_d, ssd_norm_g,
             w_out, norm2_g, w_gate, w_up, w_down, final_norm_g):
    depth = w_in.shape[0]
    a_re, a_im, bbr, bbi = _s5_params(s5_lam_re, s5_lam_im, s5_log_dt, s5_b_re, s5_b_im)
    rg = jnp.arange(S5_WIDTH)[:, None] // S5_GROUP_CH
    cg = jnp.arange(S5_LANES)[None, :] // S5_STATE
    blk = (rg == cg)[None]

    def bdiag_b(t):
        return jnp.where(blk, jnp.tile(t, (1, S5_GROUPS, 1)), 0.0)

    def bdiag_c(c):
        c2 = c.reshape(depth, S5_WIDTH, S5_STATE)
        return jnp.where(blk, jnp.tile(c2, (1, 1, S5_GROUPS)), 0.0)

    pad_l = ((0, 0), (0, 0), (0, LANES - SSD_HEADS))
    return {
        "g1": norm1_g[:, None, :], "win": _reorder_w_in(w_in),
        "a8r": a_re.reshape(depth, SUBLANES, LANES), "a8i": a_im.reshape(depth, SUBLANES, LANES),
        "bb": jnp.concatenate([bdiag_b(bbr), bdiag_b(bbi)], axis=-1).astype(BF16),
        "ct": jnp.concatenate([bdiag_c(s5_c_re), -bdiag_c(s5_c_im)], axis=-1).astype(BF16),
        "s5d": s5_d[:, None, :], "wglu": s5_w_glu.astype(BF16), "bglu": s5_b_glu[:, None, :],
        "cw": ssd_conv_w, "cb": ssd_conv_b[:, None, :],
        "dtb": jnp.pad(ssd_dt_bias[:, None, :], pad_l), "alog": jnp.pad(ssd_a_log[:, None, :], pad_l),
        "dskip": jnp.repeat(ssd_d, SSD_HEADDIM, axis=-1)[:, None, :], "ng": ssd_norm_g[:, None, :],
        "wout": w_out.astype(BF16), "g2": norm2_g[:, None, :],
        "wg": w_gate.astype(BF16), "wu": w_up.astype(BF16), "wd": w_down.astype(BF16),
        "gf": final_norm_g[None, :],
    }


def _tables(seq, chunk, t0):
    C = chunk
    lg = [math.log(1.0 - 2.0 ** (-5.0 - h)) for h in range(RET_HEADS)]
    lgv = jnp.asarray(lg, F32)
    i = jnp.arange(C, dtype=F32)
    rel = i[:, None] - i[None, :]
    dec = jnp.where(rel >= 0, jnp.exp(lgv[:, None, None] * jnp.maximum(rel, 0.0)), 0.0)
    lane = jnp.arange(RET_WIDTH)
    hq = (lane % (RET_WIDTH // 2)) // (RET_DK // 2)
    hv = lane // RET_DV
    qw = jnp.exp(lgv[hq][None, :] * (i + 1.0)[:, None])
    kw = jnp.exp(lgv[hq][None, :] * (C - 1.0 - i)[:, None])
    cdc = jnp.exp(lgv[hv] * C)[None, :]
    mbd = (hq[:, None] == hv[None, :]).astype(F32)
    mavg = jnp.where(hv[:, None] == hv[None, :], 1.0 / RET_DV, 0.0).astype(BF16)
    heads = jnp.arange(RET_HEADS)[:, None, None]
    hmq = jnp.broadcast_to(hq[None, None, :] == heads, (RET_HEADS, C, RET_WIDTH)).astype(BF16)
    hmv = jnp.broadcast_to(hv[None, None, :] == heads, (RET_HEADS, C, RET_WIDTH)).astype(BF16)
    tri = (i[:, None] >= i[None, :]).astype(BF16)
    r = jnp.arange(3 * LANES) % LANES
    e3 = (r[:, None] == (jnp.arange(SSD_WIDTH) // SSD_HEADDIM)[None, :]).astype(BF16)
    half = RET_DK // 2
    inv_freq = ROPE_BASE ** (-jnp.arange(half, dtype=F32) / half)
    pos = (t0 + jnp.arange(seq)).astype(F32)
    ang = pos[:, None] * jnp.tile(inv_freq, RET_HEADS)[None, :]
    return {"dec": dec, "qw": qw, "kw": kw, "cdc": cdc, "mbd": mbd, "mavg": mavg, "hmq": hmq, "hmv": hmv,
            "tri": tri, "e3": e3, "cos": jnp.cos(ang), "sin": jnp.sin(ang)}


def _plan(batch, seq):
    if seq >= 512:
        return (2 if batch % 2 == 0 else 1), 512, 256
    nb = max(1, min(batch, 256 // seq))
    while batch % nb:
        nb -= 1
    return nb, seq, seq


def _trunk(x, states, prm, t0):
    batch, seq, _ = x.shape
    nb, tile, chunk = _plan(batch, seq)
    tabs = _tables(seq, chunk, t0)
    depth = prm["win"].shape[0]
    x2d = x.reshape(batch * seq, D_MODEL)
    ffn_rows = min(512, batch * seq)
    outs = []
    for l in range(depth):
        x2d, st_out = _mixer_call(x2d, states, prm, tabs, l, batch=batch, seq=seq, nb=nb, tile=tile, chunk=chunk)
        x2d = _ffn_call(x2d, prm, l, rows=ffn_rows, final=(l == depth - 1))
        outs.append(st_out)
    stacked = [jnp.stack([o[i] for o in outs]) for i in range(5)]
    s5_shape = (depth, batch, S5_GROUPS, S5_STATE)
    return (x2d.reshape(batch, seq, D_MODEL),
            (stacked[0].reshape(s5_shape), stacked[1].reshape(s5_shape), stacked[2], stacked[3], stacked[4]))


def kernel(x_prompt, x_sample, state_s5_re, state_s5_im, state_ret, state_ssd, cache_ssd_conv, norm1_g, w_in, s5_lam_re, s5_lam_im, s5_log_dt, s5_b_re, s5_b_im, s5_c_re, s5_c_im, s5_d, s5_w_glu, s5_b_glu, ssd_conv_w, ssd_conv_b, ssd_dt_bias, ssd_a_log, ssd_d, ssd_norm_g, w_out, norm2_g, w_gate, w_up, w_down, final_norm_g):
    prm = _prepare(norm1_g, w_in, s5_lam_re, s5_lam_im, s5_log_dt, s5_b_re, s5_b_im, s5_c_re, s5_c_im,
                   s5_d, s5_w_glu, s5_b_glu, ssd_conv_w, ssd_conv_b, ssd_dt_bias, ssd_a_log, ssd_d, ssd_norm_g,
                   w_out, norm2_g, w_gate, w_up, w_down, final_norm_g)
    depth, bs = state_s5_re.shape[:2]
    past_len = 1024

    y_prompt, p_states = _trunk(x_prompt, None, prm, 0)
    s_in = (state_s5_re.reshape(depth, bs, SUBLANES, LANES), state_s5_im.reshape(depth, bs, SUBLANES, LANES),
            state_ret, state_ssd, cache_ssd_conv)
    y_sample, s_states = _trunk(x_sample, s_in, prm, past_len)
    return (y_prompt, y_sample) + p_states + s_states
```

```python
import functools
import math

import jax
import jax.numpy as jnp
from jax import lax
from jax.experimental import pallas as pl
from jax.experimental.pallas import tpu as pltpu

F32 = jnp.float32
BF16 = jnp.bfloat16

D_MODEL = 1024
EPS = 1e-6
ROPE_BASE = 10000.0
S5_WIDTH = 256
S5_GROUPS = 16
S5_GROUP_CH = 16
S5_STATE = 64
S5_LANES = S5_GROUPS * S5_STATE
RET_HEADS = 4
RET_DK = 64
RET_DV = 64
RET_WIDTH = RET_HEADS * RET_DV
SSD_WIDTH = 512
SSD_HEADDIM = 64
SSD_HEADS = 8
SSD_GROUPS = 2
SSD_DSTATE = 128
SSD_CONV = 4
SSD_CONV_DIM = SSD_WIDTH + 2 * SSD_GROUPS * SSD_DSTATE
D_FF = 2816
LANES = 128
SUBLANES = 8
MXU_DEPTH = 256

U0 = 0
Q0 = U0 + S5_WIDTH
K0 = Q0 + RET_HEADS * RET_DK
V0 = K0 + RET_HEADS * RET_DK
G0 = V0 + RET_WIDTH
Z0 = G0 + RET_WIDTH
X0 = Z0 + SSD_WIDTH
DT0 = X0 + SSD_CONV_DIM
IN_PAD = DT0 + LANES

CONV_TAIL = SSD_CONV - 1
CONV_LO = SUBLANES - CONV_TAIL

VMEM_LIMIT = 60000 * 1024


def _dot(a, b):
    return jnp.dot(a, b, preferred_element_type=F32)


def _dot_nt(a, b):
    return lax.dot_general(a, b, (((1,), (1,)), ((), ())), preferred_element_type=F32)


def _sigmoid(x):
    return 1.0 / (1.0 + jnp.exp(-x))


def _silu(x):
    return x * _sigmoid(x)


def _gelu_tanh(x):
    c = math.sqrt(2.0 / math.pi)
    return x * (0.5 * (1.0 + jnp.tanh(c * (x + 0.044715 * (x * x * x)))))


def _softplus(x):
    return jnp.maximum(x, 0.0) + jnp.log1p(jnp.exp(-jnp.abs(x)))


def _rms(x, g):
    ms = jnp.mean(x * x, axis=-1, keepdims=True)
    return x * lax.rsqrt(ms + EPS) * g


def _split3(x):
    hi = x.astype(BF16)
    r1 = x - hi.astype(F32)
    mid = r1.astype(BF16)
    lo = (r1 - mid.astype(F32)).astype(BF16)
    return jnp.concatenate([hi, mid, lo], axis=-1)


def _s5_param_kernel(lr_ref, li_ref, ldt_ref, br_ref, bi_ref, ar_ref, ai_ref, bbr_ref, bbi_ref):
    depth = lr_ref.shape[0]
    for l in range(depth):
        lr = lr_ref[l:l + 1, :]
        li = li_ref[l:l + 1, :]
        dt = jnp.exp(ldt_ref[l:l + 1, :])
        mag = jnp.exp(lr * dt)
        ar = mag * jnp.cos(li * dt)
        ai = mag * jnp.sin(li * dt)
        den = lr * lr + li * li
        nr = ar - 1.0
        ni = ai
        kr = (nr * lr + ni * li) / den
        ki = (ni * lr - nr * li) / den
        br = br_ref[l]
        bi = bi_ref[l]
        bbr_ref[l] = kr * br - ki * bi
        bbi_ref[l] = kr * bi + ki * br
        ar_ref[l:l + 1, :] = ar
        ai_ref[l:l + 1, :] = ai


def _s5_params(lam_re, lam_im, log_dt, b_re, b_im):
    depth = lam_re.shape[0]
    lr = lam_re.reshape(depth, S5_LANES)
    li = lam_im.reshape(depth, S5_LANES)
    ldt = jnp.repeat(log_dt, S5_STATE, axis=-1)
    brt = jnp.transpose(b_re, (0, 3, 1, 2)).reshape(depth, S5_GROUP_CH, S5_LANES)
    bit = jnp.transpose(b_im, (0, 3, 1, 2)).reshape(depth, S5_GROUP_CH, S5_LANES)
    out_shape = (jax.ShapeDtypeStruct((depth, S5_LANES), F32),
                 jax.ShapeDtypeStruct((depth, S5_LANES), F32),
                 jax.ShapeDtypeStruct((depth, S5_GROUP_CH, S5_LANES), F32),
                 jax.ShapeDtypeStruct((depth, S5_GROUP_CH, S5_LANES), F32))
    return pl.pallas_call(_s5_param_kernel, out_shape=out_shape, name="s5_params")(lr, li, ldt, brt, bit)


def _round_robin(chains, lead=None):
    chains = list(chains)
    first = True
    while chains:
        for c in list(chains):
            if next(c, StopIteration) is StopIteration:
                chains.remove(c)
        if first and lead is not None:
            for _ in lead:
                pass
        first = False


def _mixer_kernel(*refs, nb, tile, chunk, n_t, zero_init):
    (x_ref, g1_ref, win_ref, cos_ref, sin_ref,
     a8r_ref, a8i_ref, bb_ref, ct_ref, s5d_ref, wglu_ref, bglu_ref,
     dec_ref, qw_ref, kw_ref, cdc_ref, mbd_ref, mavg_ref, hmq_ref, hmv_ref, tri_ref, e3_ref,
     cw_ref, cb_ref, dtb_ref, alog_ref, dskip_ref, ng_ref, wout_ref) = refs[:29]
    refs = refs[29:]
    if not zero_init:
        s5r_in, s5i_in, ret_in, ssd_in, conv_in = refs[:5]
        refs = refs[5:]
    (xo_ref, s5r_out, s5i_out, ret_out, ssd_out, conv_out,
     proj_ref, hn_ref, mix_ref, hb_all, xp_all, ret_sc, ssd_sc, conv_sc) = refs
    unrolled = tile > chunk

    C = chunk
    nchunk = tile // C
    ic = pl.program_id(0) % n_t if n_t > 1 else 0
    half = RET_HEADS * RET_DK // 2
    hpg = SSD_HEADS // SSD_GROUPS
    gw = hpg * SSD_HEADDIM
    ret_blocks = [(h, hf, hf * half + h * (RET_DK // 2), h * RET_DV)
                  for h in range(RET_HEADS) for hf in range(2)]

    def init_states():
        if zero_init:
            s5r_out[...] = jnp.zeros_like(s5r_out)
            s5i_out[...] = jnp.zeros_like(s5i_out)
            ret_sc[...] = jnp.zeros_like(ret_sc)
            ssd_sc[...] = jnp.zeros_like(ssd_sc)
            conv_sc[...] = jnp.zeros_like(conv_sc)
            return
        s5r_out[...] = s5r_in[...]
        s5i_out[...] = s5i_in[...]
        ret_sc[...] = jnp.zeros_like(ret_sc)
        conv_sc[...] = jnp.zeros_like(conv_sc)
        for b in range(nb):
            for h, hf, r0, c0 in ret_blocks:
                ret_sc[b, r0:r0 + RET_DK // 2, c0:c0 + RET_DV] = ret_in[b, h, hf * (RET_DK // 2):(hf + 1) * (RET_DK // 2), :]
            for g in range(SSD_GROUPS):
                ssd_sc[b, g] = ssd_in[b, g * hpg:(g + 1) * hpg].reshape(gw, SSD_DSTATE).T
            conv_sc[b, CONV_LO:SUBLANES, :] = conv_in[b]

    def final_states():
        for b in range(nb):
            for h, hf, r0, c0 in ret_blocks:
                ret_out[b, h, hf * (RET_DK // 2):(hf + 1) * (RET_DK // 2), :] = ret_sc[b, r0:r0 + RET_DK // 2, c0:c0 + RET_DV]
            for g in range(SSD_GROUPS):
                ssd_out[b, g * hpg:(g + 1) * hpg] = ssd_sc[b, g].T.reshape(hpg, SSD_HEADDIM, SSD_DSTATE)
            conv_out[b] = conv_sc[b, CONV_LO:SUBLANES, :]

    if n_t > 1:
        pl.when(ic == 0)(init_states)
    else:
        init_states()

    hn_ref[...] = _rms(x_ref[...].reshape(nb * tile, D_MODEL), g1_ref[...]).astype(BF16)

    def proj_pieces(rows):
        hn = hn_ref[rows, :]
        for lo in range(0, IN_PAD, MXU_DEPTH):
            hi = min(lo + MXU_DEPTH, IN_PAD)
            proj_ref[rows, lo:hi] = _dot(hn, win_ref[:, lo:hi])
            yield

    causal = (lax.broadcasted_iota(jnp.int32, (C, C), 0) >= lax.broadcasted_iota(jnp.int32, (C, C), 1))
    lane128 = lax.broadcasted_iota(jnp.int32, (1, LANES), 1)

    def pslab(rows, lo, hi):
        return proj_ref[rows, lo:hi]

    def s5_chain(bi, ci, r0):
        rows = pl.ds(r0, C)
        hb_ref = hb_all.at[bi if unrolled else 0]
        u = pslab(rows,U0, U0 + S5_WIDTH)
        bu = _dot(u.astype(BF16), bb_ref[...])
        nslab = S5_LANES // LANES
        pitch = C + SUBLANES
        for s in range(2 * nslab):
            hb_ref[s * pitch:s * pitch + C, :] = bu[:, s * LANES:(s + 1) * LANES]
        yield
        ar = a8r_ref[...]
        ai = a8i_ref[...]
        hr = s5r_out[bi]
        hi = s5i_out[bi]
        for t in range(C):
            ld_r = pl.ds(t, nslab, stride=pitch)
            ld_i = pl.ds(nslab * pitch + t, nslab, stride=pitch)
            hr, hi = (ar * hr - ai * hi + hb_ref[ld_r, :], ar * hi + ai * hr + hb_ref[ld_i, :])
            hb_ref[ld_r, :] = hr
            hb_ref[ld_i, :] = hi
        s5r_out[bi] = hr
        s5i_out[bi] = hi
        yield
        hall = jnp.concatenate([hb_ref[s * pitch:s * pitch + C, :].astype(BF16) for s in range(2 * nslab)],
                               axis=-1)
        y = s5d_ref[...] * u + _dot_nt(hall, ct_ref[...])
        yield
        zg = _gelu_tanh(y)
        gl = _dot(zg.astype(BF16), wglu_ref[...]) + bglu_ref[...]
        yield
        mix_ref[rows, 0:S5_WIDTH] = zg * _sigmoid(gl)

    def ret_chain(bi, ci, r0):
        rows = pl.ds(r0, C)
        q1 = pslab(rows,Q0, Q0 + half)
        q2 = pslab(rows,Q0 + half, Q0 + 2 * half)
        k1 = pslab(rows,K0, K0 + half)
        k2 = pslab(rows,K0 + half, K0 + 2 * half)
        cs_ = cos_ref[pl.ds(ci * C, C), :]
        sn_ = sin_ref[pl.ds(ci * C, C), :]
        qr = jnp.concatenate([q1 * cs_ - q2 * sn_, q1 * sn_ + q2 * cs_], axis=-1)
        kr = jnp.concatenate([k1 * cs_ - k2 * sn_, k1 * sn_ + k2 * cs_], axis=-1) * (RET_DK ** -0.5)
        gate = pslab(rows,G0, G0 + RET_WIDTH)
        s_prev = ret_sc[bi]
        qb = qr.astype(BF16)
        kb = kr.astype(BF16)
        vb = pslab(rows,V0, V0 + RET_WIDTH).astype(BF16)
        o = _dot((qr * qw_ref[...]).astype(BF16), s_prev.astype(BF16))
        yield
        scs = []
        for h in range(RET_HEADS):
            scs.append(_dot_nt(qb * hmq_ref[h], kb) * dec_ref[h])
            yield
        kt = (kr * kw_ref[...]).T
        kv = _dot(kt.astype(BF16), vb)
        ret_sc[bi] = s_prev * cdc_ref[...] + kv * mbd_ref[...]
        yield
        for h in range(RET_HEADS):
            o = o + _dot(scs[h].astype(BF16), vb * hmv_ref[h])
            yield
        mavg = mavg_ref[...]
        mu = _dot(o.astype(BF16), mavg)
        yield
        dlt = o - mu
        var = _dot((dlt * dlt).astype(BF16), mavg)
        yield
        mix_ref[rows, S5_WIDTH:S5_WIDTH + RET_WIDTH] = _silu(gate) * (dlt * lax.rsqrt(var + EPS))

    def ssd_chain(bi, ci, r0):
        rows = pl.ds(r0, C)
        z = pslab(rows,Z0, Z0 + SSD_WIDTH)
        xbc = pslab(rows,X0, X0 + SSD_CONV_DIM)
        xp_ref = xp_all.at[bi if unrolled else 0]
        xp_ref[0:SUBLANES, :] = conv_sc[bi]
        xp_ref[SUBLANES:SUBLANES + C, :] = xbc
        cw = cw_ref[...]
        acc = cb_ref[...] + cw[CONV_TAIL:SSD_CONV] * xbc
        for i in range(CONV_TAIL):
            acc = acc + cw[i:i + 1] * xp_ref[CONV_LO + i:CONV_LO + i + C, :]
        conv_sc[bi] = xp_ref[C:C + SUBLANES, :]
        xc = _silu(acc)
        xs = xc[:, 0:SSD_WIDTH]
        ngl = SSD_DSTATE
        bm = xc[:, SSD_WIDTH:SSD_WIDTH + SSD_GROUPS * ngl]
        cm = xc[:, SSD_WIDTH + SSD_GROUPS * ngl:SSD_CONV_DIM]
        dt = _softplus(pslab(rows,DT0, DT0 + LANES) + dtb_ref[...])
        a_row = jnp.where(lane128 < SSD_HEADS, -jnp.exp(alog_ref[...]), 0.0)
        dta = dt * a_row
        cs3 = _dot(tri_ref[...], _split3(dta))
        yield
        cs = cs3[:, 0:LANES] + cs3[:, LANES:2 * LANES] + cs3[:, 2 * LANES:3 * LANES]
        dt_e = _dot(_split3(dt), e3_ref[...])
        yield
        cs_e = _dot(_split3(cs), e3_ref[...])
        yield
        last_e = cs_e[C - 1:C, :]
        ecs_e = jnp.exp(cs_e)
        wend_e = jnp.exp(last_e - cs_e)
        cdec_e = jnp.exp(last_e)
        xdt = xs * dt_e
        cst = cs.T
        ys = []
        for g in range(SSD_GROUPS):
            cm_g = cm[:, g * ngl:(g + 1) * ngl].astype(BF16)
            bm_g = bm[:, g * ngl:(g + 1) * ngl]
            cbm = _dot_nt(cm_g, bm_g.astype(BF16))
            yield
            st = ssd_sc[bi, g]
            xdt_g = xdt[:, g * gw:(g + 1) * gw]
            xdt_b = xdt_g.astype(BF16)
            yg = _dot(cm_g, st.astype(BF16)) * ecs_e[:, g * gw:(g + 1) * gw]
            yield
            new = _dot(bm_g.T.astype(BF16), (xdt_g * wend_e[:, g * gw:(g + 1) * gw]).astype(BF16))
            ssd_sc[bi, g] = st * cdec_e[:, g * gw:(g + 1) * gw] + new
            yield
            for hl in range(hpg):
                h = g * hpg + hl
                seg = cs[:, h:h + 1] - cst[h:h + 1, :]
                lm = jnp.where(causal, jnp.exp(seg), 0.0)
                yg = yg + _dot((cbm * lm).astype(BF16), xdt_b * hmv_ref[hl])
                yield
            ys.append(yg)
        yss = jnp.concatenate(ys, axis=-1) + dskip_ref[...] * xs
        mix_ref[rows, S5_WIDTH + RET_WIDTH:D_MODEL] = _rms(yss * _silu(z), ng_ref[...])

    def out_chain(b, t0, n):
        mixb = mix_ref[b * tile + t0:b * tile + t0 + n, :].astype(BF16)
        for lo in range(0, D_MODEL, D_MODEL // 2):
            cols = slice(lo, lo + D_MODEL // 2)
            xo_ref[b, t0:t0 + n, cols] = x_ref[b, t0:t0 + n, cols] + _dot(mixb, wout_ref[:, cols])
            yield

    def head_groups(bi, ci, r0):
        return [s5_chain(bi, ci, r0), ret_chain(bi, ci, r0), ssd_chain(bi, ci, r0)]

    def chain_all(gens):
        for g in gens:
            yield from g

    if unrolled:
        _round_robin([proj_pieces(pl.ds(b * tile, C)) for b in range(nb)])
        prev = []
        for ci in range(nchunk):
            ahead = None
            if ci + 1 < nchunk:
                ahead = chain_all([proj_pieces(pl.ds(b * tile + (ci + 1) * C, C)) for b in range(nb)])
            groups = [g for b in range(nb) for g in head_groups(b, ci, b * tile + ci * C)]
            _round_robin(groups + prev, ahead)
            prev = [out_chain(b, ci * C, C) for b in range(nb)]
        _round_robin(prev)
    else:
        proj_ref[...] = _dot(hn_ref[...], win_ref[...])

        def seg_body(j, carry):
            _round_robin(head_groups(j, 0, pl.multiple_of(j * C, C)))
            return carry
        lax.fori_loop(0, nb, seg_body, 0)
        res = x_ref[...].reshape(nb * tile, D_MODEL) + _dot(mix_ref[...].astype(BF16), wout_ref[...])
        xo_ref[...] = res.reshape(nb, tile, D_MODEL)

    if n_t > 1:
        pl.when(ic == n_t - 1)(final_states)
    else:
        final_states()


def _const_spec(shape, l=None):
    if l is None:
        return pl.BlockSpec(shape, lambda k: (0,) * len(shape), pipeline_mode=pl.Buffered(1))
    nd = len(shape)
    return pl.BlockSpec((None,) + tuple(shape), lambda k: (l,) + (0,) * nd, pipeline_mode=pl.Buffered(1))


def _mixer_call(x2d, states, prm, tabs, l, *, batch, seq, nb, tile, chunk):
    C = chunk
    n_t = seq // tile
    total = (batch // nb) * n_t
    rows = nb * tile
    zero_init = states is None
    x3d = x2d.reshape(batch, seq, D_MODEL)
    row_spec = pl.BlockSpec((nb, tile, D_MODEL), lambda k: (k // n_t, k % n_t, 0))
    ways = nb if tile > C else 1
    rope_spec = pl.BlockSpec((tile, LANES), lambda k: (k % n_t, 0))
    st_shapes = [(SUBLANES, LANES), (SUBLANES, LANES), (RET_HEADS, RET_DK, RET_DV),
                 (SSD_HEADS, SSD_HEADDIM, SSD_DSTATE), (CONV_TAIL, SSD_CONV_DIM)]
    st_out_specs = [pl.BlockSpec((nb,) + s, lambda k, _n=len(s): (k // n_t,) + (0,) * _n) for s in st_shapes]
    st_in_specs = [pl.BlockSpec((None, nb) + s, lambda k, _n=len(s): (l, k // n_t) + (0,) * _n)
                   for s in st_shapes]
    in_specs = [
        row_spec,
        _const_spec((1, D_MODEL), l),
        _const_spec((D_MODEL, IN_PAD), l),
        rope_spec,
        rope_spec,
        _const_spec((SUBLANES, LANES), l), _const_spec((SUBLANES, LANES), l),
        _const_spec((S5_WIDTH, 2 * S5_LANES), l), _const_spec((S5_WIDTH, 2 * S5_LANES), l),
        _const_spec((1, S5_WIDTH), l), _const_spec((S5_WIDTH, S5_WIDTH), l), _const_spec((1, S5_WIDTH), l),
        _const_spec((RET_HEADS, C, C)), _const_spec((C, RET_WIDTH)), _const_spec((C, RET_WIDTH)),
        _const_spec((1, RET_WIDTH)), _const_spec((RET_WIDTH, RET_WIDTH)), _const_spec((RET_WIDTH, RET_WIDTH)),
        _const_spec((RET_HEADS, C, RET_WIDTH)), _const_spec((RET_HEADS, C, RET_WIDTH)),
        _const_spec((C, C)), _const_spec((3 * LANES, SSD_WIDTH)),
        _const_spec((SSD_CONV, SSD_CONV_DIM), l), _const_spec((1, SSD_CONV_DIM), l),
        _const_spec((1, LANES), l), _const_spec((1, LANES), l),
        _const_spec((1, SSD_WIDTH), l), _const_spec((1, SSD_WIDTH), l),
        _const_spec((D_MODEL, D_MODEL), l),
    ] + ([] if zero_init else st_in_specs)
    out_specs = [row_spec] + st_out_specs
    out_shape = ([jax.ShapeDtypeStruct(x3d.shape, F32)]
                 + [jax.ShapeDtypeStruct((batch,) + s, F32) for s in st_shapes])
    scratch = [pltpu.VMEM((rows, IN_PAD), F32), pltpu.VMEM((rows, D_MODEL), BF16),
               pltpu.VMEM((rows, D_MODEL), F32),
               pltpu.VMEM((ways, 2 * S5_LANES // LANES * (C + SUBLANES), LANES), F32),
               pltpu.VMEM((ways, C + SUBLANES, SSD_CONV_DIM), F32),
               pltpu.VMEM((nb, RET_HEADS * RET_DK, RET_WIDTH), F32),
               pltpu.VMEM((nb, SSD_GROUPS, SSD_DSTATE, SSD_WIDTH // SSD_GROUPS), F32),
               pltpu.VMEM((nb, SUBLANES, SSD_CONV_DIM), F32)]
    kern = functools.partial(_mixer_kernel, nb=nb, tile=tile, chunk=C, n_t=n_t, zero_init=zero_init)
    outs = pl.pallas_call(
        kern, grid=(total,), in_specs=in_specs, out_specs=out_specs, out_shape=out_shape,
        scratch_shapes=scratch, name=f"mixer_c{C}",
        compiler_params=pltpu.CompilerParams(dimension_semantics=("arbitrary",),
                                             vmem_limit_bytes=VMEM_LIMIT),
    )(x3d, prm["g1"], prm["win"], tabs["cos"], tabs["sin"],
      prm["a8r"], prm["a8i"], prm["bb"], prm["ct"], prm["s5d"], prm["wglu"], prm["bglu"],
      tabs["dec"], tabs["qw"], tabs["kw"], tabs["cdc"], tabs["mbd"], tabs["mavg"], tabs["hmq"], tabs["hmv"],
      tabs["tri"], tabs["e3"],
      prm["cw"], prm["cb"], prm["dtb"], prm["alog"], prm["dskip"], prm["ng"], prm["wout"],
      *(() if zero_init else states))
    return outs[0].reshape(x2d.shape), tuple(outs[1:])


def _ffn_kernel(x_ref, g2_ref, wg_ref, wu_ref, wd_ref, gf_ref, o_ref, *, final, nsub):
    split = (D_FF // (2 * MXU_DEPTH) + 1) * MXU_DEPTH
    sub = x_ref.shape[0] // nsub

    def chain(rows):
        x = x_ref[rows, :]
        hn = _rms(x, g2_ref[...]).astype(BF16)
        acc = x
        for lo, hi in ((0, split), (split, D_FF)):
            gt = _dot(hn, wg_ref[:, lo:hi])
            up = _dot(hn, wu_ref[:, lo:hi])
            yield
            acc = acc + _dot((_silu(gt) * up).astype(BF16), wd_ref[lo:hi, :])
            yield
        if final:
            acc = _rms(acc, gf_ref[...])
        o_ref[rows, :] = acc

    _round_robin([chain(pl.ds(i * sub, sub)) for i in range(nsub)])


def _ffn_call(x2d, prm, l, *, rows, final):
    n = x2d.shape[0] // rows
    row_spec = pl.BlockSpec((rows, D_MODEL), lambda i: (i, 0))

    def wspec(shape, layer=True):
        nd = len(shape)
        if layer:
            return pl.BlockSpec((None,) + shape, lambda i: (l,) + (0,) * nd, pipeline_mode=pl.Buffered(1))
        return pl.BlockSpec(shape, lambda i: (0,) * nd, pipeline_mode=pl.Buffered(1))

    return pl.pallas_call(
        functools.partial(_ffn_kernel, final=final, nsub=2), grid=(n,),
        in_specs=[row_spec, wspec((1, D_MODEL)), wspec((D_MODEL, D_FF)), wspec((D_MODEL, D_FF)),
                  wspec((D_FF, D_MODEL)), wspec((1, D_MODEL), layer=False)],
        out_specs=row_spec, out_shape=jax.ShapeDtypeStruct(x2d.shape, F32),
        name="ffn_final" if final else "ffn",
        compiler_params=pltpu.CompilerParams(dimension_semantics=("arbitrary",), vmem_limit_bytes=VMEM_LIMIT),
    )(x2d, prm["g2"], prm["wg"], prm["wu"], prm["wd"], prm["gf"])


def _reorder_w_in(w_in):
    depth = w_in.shape[0]
    qk = RET_HEADS * RET_DK

    def halves_first(w):
        w = w.reshape(depth, D_MODEL, RET_HEADS, 2, RET_DK // 2)
        return w.transpose(0, 1, 3, 2, 4).reshape(depth, D_MODEL, qk)

    wb = w_in.astype(BF16)
    parts = [wb[:, :, :S5_WIDTH], halves_first(wb[:, :, S5_WIDTH:S5_WIDTH + qk]),
             halves_first(wb[:, :, S5_WIDTH + qk:S5_WIDTH + 2 * qk]), wb[:, :, S5_WIDTH + 2 * qk:]]
    used = sum(p.shape[2] for p in parts)
    parts.append(jnp.zeros((depth, D_MODEL, IN_PAD - used), BF16))
    return jnp.concatenate(parts, axis=2)


def _prepare(norm1_g, w_in, s5_lam_re, s5_lam_im, s5_log_dt, s5_b_re, s5_b_im, s5_c_re, s5_c_im,
             s5_d, s5_w_glu, s5_b_glu, ssd_conv_w, ssd_conv_b, ssd_dt_bias, ssd_a_log, ssd_d, ssd_norm_g,
             w_out, norm2_g, w_gate, w_up, w_down, final_norm_g):
    depth = w_in.shape[0]
    a_re, a_im, bbr, bbi = _s5_params(s5_lam_re, s5_lam_im, s5_log_dt, s5_b_re, s5_b_im)
    rg = jnp.arange(S5_WIDTH)[:, None] // S5_GROUP_CH
    cg = jnp.arange(S5_LANES)[None, :] // S5_STATE
    blk = (rg == cg)[None]

    def bdiag_b(t):
        return jnp.where(blk, jnp.tile(t, (1, S5_GROUPS, 1)), 0.0)

    def bdiag_c(c):
        c2 = c.reshape(depth, S5_WIDTH, S5_STATE)
        return jnp.where(blk, jnp.tile(c2, (1, 1, S5_GROUPS)), 0.0)

    pad_l = ((0, 0), (0, 0), (0, LANES - SSD_HEADS))
    return {
        "g1": norm1_g[:, None, :], "win": _reorder_w_in(w_in),
        "a8r": a_re.reshape(depth, SUBLANES, LANES), "a8i": a_im.reshape(depth, SUBLANES, LANES),
        "bb": jnp.concatenate([bdiag_b(bbr), bdiag_b(bbi)], axis=-1).astype(BF16),
        "ct": jnp.concatenate([bdiag_c(s5_c_re), -bdiag_c(s5_c_im)], axis=-1).astype(BF16),
        "s5d": s5_d[:, None, :], "wglu": s5_w_glu.astype(BF16), "bglu": s5_b_glu[:, None, :],
        "cw": ssd_conv_w, "cb": ssd_conv_b[:, None, :],
        "dtb": jnp.pad(ssd_dt_bias[:, None, :], pad_l), "alog": jnp.pad(ssd_a_log[:, None, :], pad_l),
        "dskip": jnp.repeat(ssd_d, SSD_HEADDIM, axis=-1)[:, None, :], "ng": ssd_norm_g[:, None, :],
        "wout": w_out.astype(BF16), "g2": norm2_g[:, None, :],
        "wg": w_gate.astype(BF16), "wu": w_up.astype(BF16), "wd": w_down.astype(BF16),
        "gf": final_norm_g[None, :],
    }


def _tables(seq, chunk, t0):
    C = chunk
    lg = [math.log(1.0 - 2.0 ** (-5.0 - h)) for h in range(RET_HEADS)]
    lgv = jnp.asarray(lg, F32)
    i = jnp.arange(C, dtype=F32)
    rel = i[:, None] - i[None, :]
    dec = jnp.where(rel >= 0, jnp.exp(lgv[:, None, None] * jnp.maximum(rel, 0.0)), 0.0)
    lane = jnp.arange(RET_WIDTH)
    hq = (lane % (RET_WIDTH // 2)) // (RET_DK // 2)
    hv = lane // RET_DV
    qw = jnp.exp(lgv[hq][None, :] * (i + 1.0)[:, None])
    kw = jnp.exp(lgv[hq][None, :] * (C - 1.0 - i)[:, None])
    cdc = jnp.exp(lgv[hv] * C)[None, :]
    mbd = (hq[:, None] == hv[None, :]).astype(F32)
    mavg = jnp.where(hv[:, None] == hv[None, :], 1.0 / RET_DV, 0.0).astype(BF16)
    heads = jnp.arange(RET_HEADS)[:, None, None]
    hmq = jnp.broadcast_to(hq[None, None, :] == heads, (RET_HEADS, C, RET_WIDTH)).astype(BF16)
    hmv = jnp.broadcast_to(hv[None, None, :] == heads, (RET_HEADS, C, RET_WIDTH)).astype(BF16)
    tri = (i[:, None] >= i[None, :]).astype(BF16)
    r = jnp.arange(3 * LANES) % LANES
    e3 = (r[:, None] == (jnp.arange(SSD_WIDTH) // SSD_HEADDIM)[None, :]).astype(BF16)
    half = RET_DK // 2
    inv_freq = ROPE_BASE ** (-jnp.arange(half, dtype=F32) / half)
    pos = (t0 + jnp.arange(seq)).astype(F32)
    ang = pos[:, None] * jnp.tile(inv_freq, RET_HEADS)[None, :]
    return {"dec": dec, "qw": qw, "kw": kw, "cdc": cdc, "mbd": mbd, "mavg": mavg, "hmq": hmq, "hmv": hmv,
            "tri": tri, "e3": e3, "cos": jnp.cos(ang), "sin": jnp.sin(ang)}


def _plan(batch, seq):
    if seq >= 512:
        return (2 if batch % 2 == 0 else 1), 512, 256
    nb = max(1, min(batch, 256 // seq))
    while batch % nb:
        nb -= 1
    return nb, seq, seq


def _trunk(x, states, prm, t0):
    batch, seq, _ = x.shape
    nb, tile, chunk = _plan(batch, seq)
    tabs = _tables(seq, chunk, t0)
    depth = prm["win"].shape[0]
    x2d = x.reshape(batch * seq, D_MODEL)
    ffn_rows = min(1024, batch * seq)
    outs = []
    for l in range(depth):
        x2d, st_out = _mixer_call(x2d, states, prm, tabs, l, batch=batch, seq=seq, nb=nb, tile=tile, chunk=chunk)
        x2d = _ffn_call(x2d, prm, l, rows=ffn_rows, final=(l == depth - 1))
        outs.append(st_out)
    stacked = [jnp.stack([o[i] for o in outs]) for i in range(5)]
    s5_shape = (depth, batch, S5_GROUPS, S5_STATE)
    return (x2d.reshape(batch, seq, D_MODEL),
            (stacked[0].reshape(s5_shape), stacked[1].reshape(s5_shape), stacked[2], stacked[3], stacked[4]))


def kernel(x_prompt, x_sample, state_s5_re, state_s5_im, state_ret, state_ssd, cache_ssd_conv, norm1_g, w_in, s5_lam_re, s5_lam_im, s5_log_dt, s5_b_re, s5_b_im, s5_c_re, s5_c_im, s5_d, s5_w_glu, s5_b_glu, ssd_conv_w, ssd_conv_b, ssd_dt_bias, ssd_a_log, ssd_d, ssd_norm_g, w_out, norm2_g, w_gate, w_up, w_down, final_norm_g):
    prm = _prepare(norm1_g, w_in, s5_lam_re, s5_lam_im, s5_log_dt, s5_b_re, s5_b_im, s5_c_re, s5_c_im,
                   s5_d, s5_w_glu, s5_b_glu, ssd_conv_w, ssd_conv_b, ssd_dt_bias, ssd_a_log, ssd_d, ssd_norm_g,
                   w_out, norm2_g, w_gate, w_up, w_down, final_norm_g)
    depth, bs = state_s5_re.shape[:2]
    past_len = 1024

    y_prompt, p_states = _trunk(x_prompt, None, prm, 0)
    s_in = (state_s5_re.reshape(depth, bs, SUBLANES, LANES), state_s5_im.reshape(depth, bs, SUBLANES, LANES),
            state_ret, state_ssd, cache_ssd_conv)
    y_sample, s_states = _trunk(x_sample, s_in, prm, past_len)
    return (y_prompt, y_sample) + p_states + s_states
```

```python
import functools
import math

import jax
import jax.numpy as jnp
from jax import lax
from jax.experimental import pallas as pl
from jax.experimental.pallas import tpu as pltpu

F32 = jnp.float32
BF16 = jnp.bfloat16

D_MODEL = 1024
EPS = 1e-6
ROPE_BASE = 10000.0
S5_WIDTH = 256
S5_GROUPS = 16
S5_GROUP_CH = 16
S5_STATE = 64
S5_LANES = S5_GROUPS * S5_STATE
RET_HEADS = 4
RET_DK = 64
RET_DV = 64
RET_WIDTH = RET_HEADS * RET_DV
SSD_WIDTH = 512
SSD_HEADDIM = 64
SSD_HEADS = 8
SSD_GROUPS = 2
SSD_DSTATE = 128
SSD_CONV = 4
SSD_CONV_DIM = SSD_WIDTH + 2 * SSD_GROUPS * SSD_DSTATE
D_FF = 2816
LANES = 128
SUBLANES = 8
MXU_DEPTH = 256

U0 = 0
Q0 = U0 + S5_WIDTH
K0 = Q0 + RET_HEADS * RET_DK
V0 = K0 + RET_HEADS * RET_DK
G0 = V0 + RET_WIDTH
Z0 = G0 + RET_WIDTH
X0 = Z0 + SSD_WIDTH
DT0 = X0 + SSD_CONV_DIM
IN_PAD = DT0 + LANES

CONV_TAIL = SSD_CONV - 1
CONV_LO = SUBLANES - CONV_TAIL

VMEM_LIMIT = 60000 * 1024


def _dot(a, b):
    return jnp.dot(a, b, preferred_element_type=F32)


def _dot_nt(a, b):
    return lax.dot_general(a, b, (((1,), (1,)), ((), ())), preferred_element_type=F32)


def _sigmoid(x):
    return 1.0 / (1.0 + jnp.exp(-x))


def _silu(x):
    return x * _sigmoid(x)


def _gelu_tanh(x):
    c = math.sqrt(2.0 / math.pi)
    return x * (0.5 * (1.0 + jnp.tanh(c * (x + 0.044715 * (x * x * x)))))


def _softplus(x):
    return jnp.maximum(x, 0.0) + jnp.log1p(jnp.exp(-jnp.abs(x)))


def _rms(x, g):
    ms = jnp.mean(x * x, axis=-1, keepdims=True)
    return x * lax.rsqrt(ms + EPS) * g


NSPLIT = 2


def _split(x):
    parts, r = [], x
    for i in range(NSPLIT):
        p = r.astype(BF16)
        parts.append(p)
        if i + 1 < NSPLIT:
            r = r - p.astype(F32)
    return jnp.concatenate(parts, axis=-1)


def _unsplit(y):
    w = y.shape[-1] // NSPLIT
    out = y[:, 0:w]
    for i in range(1, NSPLIT):
        out = out + y[:, i * w:(i + 1) * w]
    return out


def _s5_param_kernel(lr_ref, li_ref, ldt_ref, br_ref, bi_ref, ar_ref, ai_ref, bbr_ref, bbi_ref):
    depth = lr_ref.shape[0]
    for l in range(depth):
        lr = lr_ref[l:l + 1, :]
        li = li_ref[l:l + 1, :]
        dt = jnp.exp(ldt_ref[l:l + 1, :])
        mag = jnp.exp(lr * dt)
        ar = mag * jnp.cos(li * dt)
        ai = mag * jnp.sin(li * dt)
        den = lr * lr + li * li
        nr = ar - 1.0
        ni = ai
        kr = (nr * lr + ni * li) / den
        ki = (ni * lr - nr * li) / den
        br = br_ref[l]
        bi = bi_ref[l]
        bbr_ref[l] = kr * br - ki * bi
        bbi_ref[l] = kr * bi + ki * br
        ar_ref[l:l + 1, :] = ar
        ai_ref[l:l + 1, :] = ai


def _s5_params(lam_re, lam_im, log_dt, b_re, b_im):
    depth = lam_re.shape[0]
    lr = lam_re.reshape(depth, S5_LANES)
    li = lam_im.reshape(depth, S5_LANES)
    ldt = jnp.repeat(log_dt, S5_STATE, axis=-1)
    brt = jnp.transpose(b_re, (0, 3, 1, 2)).reshape(depth, S5_GROUP_CH, S5_LANES)
    bit = jnp.transpose(b_im, (0, 3, 1, 2)).reshape(depth, S5_GROUP_CH, S5_LANES)
    out_shape = (jax.ShapeDtypeStruct((depth, S5_LANES), F32),
                 jax.ShapeDtypeStruct((depth, S5_LANES), F32),
                 jax.ShapeDtypeStruct((depth, S5_GROUP_CH, S5_LANES), F32),
                 jax.ShapeDtypeStruct((depth, S5_GROUP_CH, S5_LANES), F32))
    return pl.pallas_call(_s5_param_kernel, out_shape=out_shape, name="s5_params")(lr, li, ldt, brt, bit)


def _round_robin(chains, lead=None):
    chains = list(chains)
    first = True
    while chains:
        for c in list(chains):
            if next(c, StopIteration) is StopIteration:
                chains.remove(c)
        if first and lead is not None:
            for _ in lead:
                pass
        first = False


def _mixer_kernel(*refs, nb, tile, chunk, n_t, zero_init):
    (x_ref, g1_ref, win_ref, cos_ref, sin_ref,
     a8r_ref, a8i_ref, bb_ref, ct_ref, s5d_ref, wglu_ref, bglu_ref,
     dec_ref, qw_ref, kw_ref, cdc_ref, mbd_ref, mavg_ref, hmq_ref, hmv_ref, tri_ref, e3_ref,
     cw_ref, cb_ref, dtb_ref, alog_ref, dskip_ref, ng_ref, wout_ref) = refs[:29]
    refs = refs[29:]
    if not zero_init:
        s5r_in, s5i_in, ret_in, ssd_in, conv_in = refs[:5]
        refs = refs[5:]
    (xo_ref, s5r_out, s5i_out, ret_out, ssd_out, conv_out,
     proj_ref, hn_ref, mix_ref, hb_all, xp_all, ret_sc, ssd_sc, conv_sc) = refs
    unrolled = tile > chunk

    C = chunk
    nchunk = tile // C
    ic = pl.program_id(0) % n_t if n_t > 1 else 0
    half = RET_HEADS * RET_DK // 2
    hpg = SSD_HEADS // SSD_GROUPS
    gw = hpg * SSD_HEADDIM
    ret_blocks = [(h, hf, hf * half + h * (RET_DK // 2), h * RET_DV)
                  for h in range(RET_HEADS) for hf in range(2)]

    def init_states():
        if zero_init:
            s5r_out[...] = jnp.zeros_like(s5r_out)
            s5i_out[...] = jnp.zeros_like(s5i_out)
            ret_sc[...] = jnp.zeros_like(ret_sc)
            ssd_sc[...] = jnp.zeros_like(ssd_sc)
            conv_sc[...] = jnp.zeros_like(conv_sc)
            return
        s5r_out[...] = s5r_in[...]
        s5i_out[...] = s5i_in[...]
        ret_sc[...] = jnp.zeros_like(ret_sc)
        conv_sc[...] = jnp.zeros_like(conv_sc)
        for b in range(nb):
            for h, hf, r0, c0 in ret_blocks:
                ret_sc[b, r0:r0 + RET_DK // 2, c0:c0 + RET_DV] = ret_in[b, h, hf * (RET_DK // 2):(hf + 1) * (RET_DK // 2), :]
            for g in range(SSD_GROUPS):
                ssd_sc[b, g] = ssd_in[b, g * hpg:(g + 1) * hpg].reshape(gw, SSD_DSTATE).T
            conv_sc[b, CONV_LO:SUBLANES, :] = conv_in[b]

    def final_states():
        for b in range(nb):
            for h, hf, r0, c0 in ret_blocks:
                ret_out[b, h, hf * (RET_DK // 2):(hf + 1) * (RET_DK // 2), :] = ret_sc[b, r0:r0 + RET_DK // 2, c0:c0 + RET_DV]
            for g in range(SSD_GROUPS):
                ssd_out[b, g * hpg:(g + 1) * hpg] = ssd_sc[b, g].T.reshape(hpg, SSD_HEADDIM, SSD_DSTATE)
            conv_out[b] = conv_sc[b, CONV_LO:SUBLANES, :]

    if n_t > 1:
        pl.when(ic == 0)(init_states)
    else:
        init_states()

    hn_ref[...] = _rms(x_ref[...].reshape(nb * tile, D_MODEL), g1_ref[...]).astype(BF16)

    def proj_pieces(rows):
        hn = hn_ref[rows, :]
        for lo in range(0, IN_PAD, MXU_DEPTH):
            hi = min(lo + MXU_DEPTH, IN_PAD)
            proj_ref[rows, lo:hi] = _dot(hn, win_ref[:, lo:hi])
            yield

    causal = (lax.broadcasted_iota(jnp.int32, (C, C), 0) >= lax.broadcasted_iota(jnp.int32, (C, C), 1))
    lane128 = lax.broadcasted_iota(jnp.int32, (1, LANES), 1)

    def pslab(rows, lo, hi):
        return proj_ref[rows, lo:hi]

    def s5_chain(bi, ci, r0):
        rows = pl.ds(r0, C)
        hb_ref = hb_all.at[bi if unrolled else 0]
        u = pslab(rows,U0, U0 + S5_WIDTH)
        bu = _dot(u.astype(BF16), bb_ref[...])
        nslab = S5_LANES // LANES
        pitch = C + SUBLANES
        for s in range(2 * nslab):
            hb_ref[s * pitch:s * pitch + C, :] = bu[:, s * LANES:(s + 1) * LANES]
        yield
        ar = a8r_ref[...]
        ai = a8i_ref[...]
        hr = s5r_out[bi]
        hi = s5i_out[bi]
        for t in range(C):
            ld_r = pl.ds(t, nslab, stride=pitch)
            ld_i = pl.ds(nslab * pitch + t, nslab, stride=pitch)
            hr, hi = (ar * hr - ai * hi + hb_ref[ld_r, :], ar * hi + ai * hr + hb_ref[ld_i, :])
            hb_ref[ld_r, :] = hr
            hb_ref[ld_i, :] = hi
        s5r_out[bi] = hr
        s5i_out[bi] = hi
        yield
        hall = jnp.concatenate([hb_ref[s * pitch:s * pitch + C, :].astype(BF16) for s in range(2 * nslab)],
                               axis=-1)
        y = s5d_ref[...] * u + _dot_nt(hall, ct_ref[...])
        yield
        zg = _gelu_tanh(y)
        gl = _dot(zg.astype(BF16), wglu_ref[...]) + bglu_ref[...]
        yield
        mix_ref[rows, 0:S5_WIDTH] = zg * _sigmoid(gl)

    def ret_chain(bi, ci, r0):
        rows = pl.ds(r0, C)
        q1 = pslab(rows,Q0, Q0 + half)
        q2 = pslab(rows,Q0 + half, Q0 + 2 * half)
        k1 = pslab(rows,K0, K0 + half)
        k2 = pslab(rows,K0 + half, K0 + 2 * half)
        cs_ = cos_ref[pl.ds(ci * C, C), :]
        sn_ = sin_ref[pl.ds(ci * C, C), :]
        qr = jnp.concatenate([q1 * cs_ - q2 * sn_, q1 * sn_ + q2 * cs_], axis=-1)
        kr = jnp.concatenate([k1 * cs_ - k2 * sn_, k1 * sn_ + k2 * cs_], axis=-1) * (RET_DK ** -0.5)
        gate = pslab(rows,G0, G0 + RET_WIDTH)
        s_prev = ret_sc[bi]
        qb = qr.astype(BF16)
        kb = kr.astype(BF16)
        vb = pslab(rows,V0, V0 + RET_WIDTH).astype(BF16)
        o = _dot((qr * qw_ref[...]).astype(BF16), s_prev.astype(BF16))
        yield
        scs = []
        for h in range(RET_HEADS):
            scs.append(_dot_nt(qb * hmq_ref[h], kb) * dec_ref[h])
            yield
        kt = (kr * kw_ref[...]).T
        kv = _dot(kt.astype(BF16), vb)
        ret_sc[bi] = s_prev * cdc_ref[...] + kv * mbd_ref[...]
        yield
        for h in range(RET_HEADS):
            o = o + _dot(scs[h].astype(BF16), vb * hmv_ref[h])
            yield
        mavg = mavg_ref[...]
        mu = _dot(o.astype(BF16), mavg)
        yield
        dlt = o - mu
        var = _dot((dlt * dlt).astype(BF16), mavg)
        yield
        mix_ref[rows, S5_WIDTH:S5_WIDTH + RET_WIDTH] = _silu(gate) * (dlt * lax.rsqrt(var + EPS))

    def ssd_chain(bi, ci, r0):
        rows = pl.ds(r0, C)
        z = pslab(rows,Z0, Z0 + SSD_WIDTH)
        xbc = pslab(rows,X0, X0 + SSD_CONV_DIM)
        xp_ref = xp_all.at[bi if unrolled else 0]
        xp_ref[0:SUBLANES, :] = conv_sc[bi]
        xp_ref[SUBLANES:SUBLANES + C, :] = xbc
        cw = cw_ref[...]
        acc = cb_ref[...] + cw[CONV_TAIL:SSD_CONV] * xbc
        for i in range(CONV_TAIL):
            acc = acc + cw[i:i + 1] * xp_ref[CONV_LO + i:CONV_LO + i + C, :]
        conv_sc[bi] = xp_ref[C:C + SUBLANES, :]
        xc = _silu(acc)
        xs = xc[:, 0:SSD_WIDTH]
        ngl = SSD_DSTATE
        bm = xc[:, SSD_WIDTH:SSD_WIDTH + SSD_GROUPS * ngl]
        cm = xc[:, SSD_WIDTH + SSD_GROUPS * ngl:SSD_CONV_DIM]
        dt = _softplus(pslab(rows,DT0, DT0 + LANES) + dtb_ref[...])
        a_row = jnp.where(lane128 < SSD_HEADS, -jnp.exp(alog_ref[...]), 0.0)
        dta = dt * a_row
        cs_parts = _dot(tri_ref[...], _split(dta))
        yield
        cs = _unsplit(cs_parts)
        dt_e = _dot(_split(dt), e3_ref[...])
        yield
        cs_e = _dot(_split(cs), e3_ref[...])
        yield
        last_e = cs_e[C - 1:C, :]
        ecs_e = jnp.exp(cs_e)
        wend_e = jnp.exp(last_e - cs_e)
        cdec_e = jnp.exp(last_e)
        xdt = xs * dt_e
        cst = cs.T
        ys = []
        for g in range(SSD_GROUPS):
            cm_g = cm[:, g * ngl:(g + 1) * ngl].astype(BF16)
            bm_g = bm[:, g * ngl:(g + 1) * ngl]
            cbm = _dot_nt(cm_g, bm_g.astype(BF16))
            yield
            st = ssd_sc[bi, g]
            xdt_g = xdt[:, g * gw:(g + 1) * gw]
            xdt_b = xdt_g.astype(BF16)
            yg = _dot(cm_g, st.astype(BF16)) * ecs_e[:, g * gw:(g + 1) * gw]
            yield
            new = _dot(bm_g.T.astype(BF16), (xdt_g * wend_e[:, g * gw:(g + 1) * gw]).astype(BF16))
            ssd_sc[bi, g] = st * cdec_e[:, g * gw:(g + 1) * gw] + new
            yield
            for hl in range(hpg):
                h = g * hpg + hl
                seg = cs[:, h:h + 1] - cst[h:h + 1, :]
                lm = jnp.where(causal, jnp.exp(seg), 0.0)
                yg = yg + _dot((cbm * lm).astype(BF16), xdt_b * hmv_ref[hl])
                yield
            ys.append(yg)
        yss = jnp.concatenate(ys, axis=-1) + dskip_ref[...] * xs
        mix_ref[rows, S5_WIDTH + RET_WIDTH:D_MODEL] = _rms(yss * _silu(z), ng_ref[...])

    def out_chain(b, t0, n):
        mixb = mix_ref[b * tile + t0:b * tile + t0 + n, :].astype(BF16)
        for lo in range(0, D_MODEL, D_MODEL // 2):
            cols = slice(lo, lo + D_MODEL // 2)
            xo_ref[b, t0:t0 + n, cols] = x_ref[b, t0:t0 + n, cols] + _dot(mixb, wout_ref[:, cols])
            yield

    def head_groups(bi, ci, r0):
        return [s5_chain(bi, ci, r0), ret_chain(bi, ci, r0), ssd_chain(bi, ci, r0)]

    def chain_all(gens):
        for g in gens:
            yield from g

    if unrolled:
        _round_robin([proj_pieces(pl.ds(b * tile, C)) for b in range(nb)])
        prev = []
        for ci in range(nchunk):
            ahead = None
            if ci + 1 < nchunk:
                ahead = chain_all([proj_pieces(pl.ds(b * tile + (ci + 1) * C, C)) for b in range(nb)])
            groups = [g for b in range(nb) for g in head_groups(b, ci, b * tile + ci * C)]
            _round_robin(groups + prev, ahead)
            prev = [out_chain(b, ci * C, C) for b in range(nb)]
        _round_robin(prev)
    else:
        proj_ref[...] = _dot(hn_ref[...], win_ref[...])

        def seg_body(j, carry):
            _round_robin(head_groups(j, 0, pl.multiple_of(j * C, C)))
            return carry
        lax.fori_loop(0, nb, seg_body, 0)
        res = x_ref[...].reshape(nb * tile, D_MODEL) + _dot(mix_ref[...].astype(BF16), wout_ref[...])
        xo_ref[...] = res.reshape(nb, tile, D_MODEL)

    if n_t > 1:
        pl.when(ic == n_t - 1)(final_states)
    else:
        final_states()


def _const_spec(shape, l=None):
    if l is None:
        return pl.BlockSpec(shape, lambda k: (0,) * len(shape), pipeline_mode=pl.Buffered(1))
    nd = len(shape)
    return pl.BlockSpec((None,) + tuple(shape), lambda k: (l,) + (0,) * nd, pipeline_mode=pl.Buffered(1))


def _mixer_call(x2d, states, prm, tabs, l, *, batch, seq, nb, tile, chunk):
    C = chunk
    n_t = seq // tile
    total = (batch // nb) * n_t
    rows = nb * tile
    zero_init = states is None
    x3d = x2d.reshape(batch, seq, D_MODEL)
    row_spec = pl.BlockSpec((nb, tile, D_MODEL), lambda k: (k // n_t, k % n_t, 0))
    ways = nb if tile > C else 1
    rope_spec = pl.BlockSpec((tile, LANES), lambda k: (k % n_t, 0))
    st_shapes = [(SUBLANES, LANES), (SUBLANES, LANES), (RET_HEADS, RET_DK, RET_DV),
                 (SSD_HEADS, SSD_HEADDIM, SSD_DSTATE), (CONV_TAIL, SSD_CONV_DIM)]
    st_out_specs = [pl.BlockSpec((nb,) + s, lambda k, _n=len(s): (k // n_t,) + (0,) * _n) for s in st_shapes]
    st_in_specs = [pl.BlockSpec((None, nb) + s, lambda k, _n=len(s): (l, k // n_t) + (0,) * _n)
                   for s in st_shapes]
    in_specs = [
        row_spec,
        _const_spec((1, D_MODEL), l),
        _const_spec((D_MODEL, IN_PAD), l),
        rope_spec,
        rope_spec,
        _const_spec((SUBLANES, LANES), l), _const_spec((SUBLANES, LANES), l),
        _const_spec((S5_WIDTH, 2 * S5_LANES), l), _const_spec((S5_WIDTH, 2 * S5_LANES), l),
        _const_spec((1, S5_WIDTH), l), _const_spec((S5_WIDTH, S5_WIDTH), l), _const_spec((1, S5_WIDTH), l),
        _const_spec((RET_HEADS, C, C)), _const_spec((C, RET_WIDTH)), _const_spec((C, RET_WIDTH)),
        _const_spec((1, RET_WIDTH)), _const_spec((RET_WIDTH, RET_WIDTH)), _const_spec((RET_WIDTH, RET_WIDTH)),
        _const_spec((RET_HEADS, C, RET_WIDTH)), _const_spec((RET_HEADS, C, RET_WIDTH)),
        _const_spec((C, C)), _const_spec((NSPLIT * LANES, SSD_WIDTH)),
        _const_spec((SSD_CONV, SSD_CONV_DIM), l), _const_spec((1, SSD_CONV_DIM), l),
        _const_spec((1, LANES), l), _const_spec((1, LANES), l),
        _const_spec((1, SSD_WIDTH), l), _const_spec((1, SSD_WIDTH), l),
        _const_spec((D_MODEL, D_MODEL), l),
    ] + ([] if zero_init else st_in_specs)
    out_specs = [row_spec] + st_out_specs
    out_shape = ([jax.ShapeDtypeStruct(x3d.shape, F32)]
                 + [jax.ShapeDtypeStruct((batch,) + s, F32) for s in st_shapes])
    scratch = [pltpu.VMEM((rows, IN_PAD), F32), pltpu.VMEM((rows, D_MODEL), BF16),
               pltpu.VMEM((rows, D_MODEL), F32),
               pltpu.VMEM((ways, 2 * S5_LANES // LANES * (C + SUBLANES), LANES), F32),
               pltpu.VMEM((ways, C + SUBLANES, SSD_CONV_DIM), F32),
               pltpu.VMEM((nb, RET_HEADS * RET_DK, RET_WIDTH), F32),
               pltpu.VMEM((nb, SSD_GROUPS, SSD_DSTATE, SSD_WIDTH // SSD_GROUPS), F32),
               pltpu.VMEM((nb, SUBLANES, SSD_CONV_DIM), F32)]
    kern = functools.partial(_mixer_kernel, nb=nb, tile=tile, chunk=C, n_t=n_t, zero_init=zero_init)
    outs = pl.pallas_call(
        kern, grid=(total,), in_specs=in_specs, out_specs=out_specs, out_shape=out_shape,
        scratch_shapes=scratch, name=f"mixer_c{C}",
        compiler_params=pltpu.CompilerParams(dimension_semantics=("arbitrary",),
                                             vmem_limit_bytes=VMEM_LIMIT),
    )(x3d, prm["g1"], prm["win"], tabs["cos"], tabs["sin"],
      prm["a8r"], prm["a8i"], prm["bb"], prm["ct"], prm["s5d"], prm["wglu"], prm["bglu"],
      tabs["dec"], tabs["qw"], tabs["kw"], tabs["cdc"], tabs["mbd"], tabs["mavg"], tabs["hmq"], tabs["hmv"],
      tabs["tri"], tabs["e3"],
      prm["cw"], prm["cb"], prm["dtb"], prm["alog"], prm["dskip"], prm["ng"], prm["wout"],
      *(() if zero_init else states))
    return outs[0].reshape(x2d.shape), tuple(outs[1:])


def _ffn_kernel(x_ref, g2_ref, wg_ref, wu_ref, wd_ref, gf_ref, o_ref, *, final, nsub):
    split = (D_FF // (2 * MXU_DEPTH) + 1) * MXU_DEPTH
    sub = x_ref.shape[0] // nsub

    def chain(rows):
        x = x_ref[rows, :]
        hn = _rms(x, g2_ref[...]).astype(BF16)
        acc = x
        for lo, hi in ((0, split), (split, D_FF)):
            gt = _dot(hn, wg_ref[:, lo:hi])
            up = _dot(hn, wu_ref[:, lo:hi])
            yield
            acc = acc + _dot((_silu(gt) * up).astype(BF16), wd_ref[lo:hi, :])
            yield
        if final:
            acc = _rms(acc, gf_ref[...])
        o_ref[rows, :] = acc

    _round_robin([chain(pl.ds(i * sub, sub)) for i in range(nsub)])


def _ffn_call(x2d, prm, l, *, rows, final):
    n = x2d.shape[0] // rows
    row_spec = pl.BlockSpec((rows, D_MODEL), lambda i: (i, 0))

    def wspec(shape, layer=True):
        nd = len(shape)
        if layer:
            return pl.BlockSpec((None,) + shape, lambda i: (l,) + (0,) * nd, pipeline_mode=pl.Buffered(1))
        return pl.BlockSpec(shape, lambda i: (0,) * nd, pipeline_mode=pl.Buffered(1))

    return pl.pallas_call(
        functools.partial(_ffn_kernel, final=final, nsub=2), grid=(n,),
        in_specs=[row_spec, wspec((1, D_MODEL)), wspec((D_MODEL, D_FF)), wspec((D_MODEL, D_FF)),
                  wspec((D_FF, D_MODEL)), wspec((1, D_MODEL), layer=False)],
        out_specs=row_spec, out_shape=jax.ShapeDtypeStruct(x2d.shape, F32),
        name="ffn_final" if final else "ffn",
        compiler_params=pltpu.CompilerParams(dimension_semantics=("arbitrary",), vmem_limit_bytes=VMEM_LIMIT),
    )(x2d, prm["g2"], prm["wg"], prm["wu"], prm["wd"], prm["gf"])


def _reorder_w_in(w_in):
    depth = w_in.shape[0]
    qk = RET_HEADS * RET_DK

    def halves_first(w):
        w = w.reshape(depth, D_MODEL, RET_HEADS, 2, RET_DK // 2)
        return w.transpose(0, 1, 3, 2, 4).reshape(depth, D_MODEL, qk)

    wb = w_in.astype(BF16)
    parts = [wb[:, :, :S5_WIDTH], halves_first(wb[:, :, S5_WIDTH:S5_WIDTH + qk]),
             halves_first(wb[:, :, S5_WIDTH + qk:S5_WIDTH + 2 * qk]), wb[:, :, S5_WIDTH + 2 * qk:]]
    used = sum(p.shape[2] for p in parts)
    parts.append(jnp.zeros((depth, D_MODEL, IN_PAD - used), BF16))
    return jnp.concatenate(parts, axis=2)


def _prepare(norm1_g, w_in, s5_lam_re, s5_lam_im, s5_log_dt, s5_b_re, s5_b_im, s5_c_re, s5_c_im,
             s5_d, s5_w_glu, s5_b_glu, ssd_conv_w, ssd_conv_b, ssd_dt_bias, ssd_a_log, ssd_d, ssd_norm_g,
             w_out, norm2_g, w_gate, w_up, w_down, final_norm_g):
    depth = w_in.shape[0]
    a_re, a_im, bbr, bbi = _s5_params(s5_lam_re, s5_lam_im, s5_log_dt, s5_b_re, s5_b_im)
    rg = jnp.arange(S5_WIDTH)[:, None] // S5_GROUP_CH
    cg = jnp.arange(S5_LANES)[None, :] // S5_STATE
    blk = (rg == cg)[None]

    def bdiag_b(t):
        return jnp.where(blk, jnp.tile(t, (1, S5_GROUPS, 1)), 0.0)

    def bdiag_c(c):
        c2 = c.reshape(depth, S5_WIDTH, S5_STATE)
        return jnp.where(blk, jnp.tile(c2, (1, 1, S5_GROUPS)), 0.0)

    pad_l = ((0, 0), (0, 0), (0, LANES - SSD_HEADS))
    return {
        "g1": norm1_g[:, None, :], "win": _reorder_w_in(w_in),
        "a8r": a_re.reshape(depth, SUBLANES, LANES), "a8i": a_im.reshape(depth, SUBLANES, LANES),
        "bb": jnp.concatenate([bdiag_b(bbr), bdiag_b(bbi)], axis=-1).astype(BF16),
        "ct": jnp.concatenate([bdiag_c(s5_c_re), -bdiag_c(s5_c_im)], axis=-1).astype(BF16),
        "s5d": s5_d[:, None, :], "wglu": s5_w_glu.astype(BF16), "bglu": s5_b_glu[:, None, :],
        "cw": ssd_conv_w, "cb": ssd_conv_b[:, None, :],
        "dtb": jnp.pad(ssd_dt_bias[:, None, :], pad_l), "alog": jnp.pad(ssd_a_log[:, None, :], pad_l),
        "dskip": jnp.repeat(ssd_d, SSD_HEADDIM, axis=-1)[:, None, :], "ng": ssd_norm_g[:, None, :],
        "wout": w_out.astype(BF16), "g2": norm2_g[:, None, :],
        "wg": w_gate.astype(BF16), "wu": w_up.astype(BF16), "wd": w_down.astype(BF16),
        "gf": final_norm_g[None, :],
    }


def _tables(seq, chunk, t0):
    C = chunk
    lg = [math.log(1.0 - 2.0 ** (-5.0 - h)) for h in range(RET_HEADS)]
    lgv = jnp.asarray(lg, F32)
    i = jnp.arange(C, dtype=F32)
    rel = i[:, None] - i[None, :]
    dec = jnp.where(rel >= 0, jnp.exp(lgv[:, None, None] * jnp.maximum(rel, 0.0)), 0.0)
    lane = jnp.arange(RET_WIDTH)
    hq = (lane % (RET_WIDTH // 2)) // (RET_DK // 2)
    hv = lane // RET_DV
    qw = jnp.exp(lgv[hq][None, :] * (i + 1.0)[:, None])
    kw = jnp.exp(lgv[hq][None, :] * (C - 1.0 - i)[:, None])
    cdc = jnp.exp(lgv[hv] * C)[None, :]
    mbd = (hq[:, None] == hv[None, :]).astype(F32)
    mavg = jnp.where(hv[:, None] == hv[None, :], 1.0 / RET_DV, 0.0).astype(BF16)
    heads = jnp.arange(RET_HEADS)[:, None, None]
    hmq = jnp.broadcast_to(hq[None, None, :] == heads, (RET_HEADS, C, RET_WIDTH)).astype(BF16)
    hmv = jnp.broadcast_to(hv[None, None, :] == heads, (RET_HEADS, C, RET_WIDTH)).astype(BF16)
    tri = (i[:, None] >= i[None, :]).astype(BF16)
    r = jnp.arange(NSPLIT * LANES) % LANES
    e3 = (r[:, None] == (jnp.arange(SSD_WIDTH) // SSD_HEADDIM)[None, :]).astype(BF16)
    half = RET_DK // 2
    inv_freq = ROPE_BASE ** (-jnp.arange(half, dtype=F32) / half)
    pos = (t0 + jnp.arange(seq)).astype(F32)
    ang = pos[:, None] * jnp.tile(inv_freq, RET_HEADS)[None, :]
    return {"dec": dec, "qw": qw, "kw": kw, "cdc": cdc, "mbd": mbd, "mavg": mavg, "hmq": hmq, "hmv": hmv,
            "tri": tri, "e3": e3, "cos": jnp.cos(ang), "sin": jnp.sin(ang)}


def _plan(batch, seq):
    if seq >= 512:
        return (2 if batch % 2 == 0 else 1), 512, 256
    nb = max(1, min(batch, 256 // seq))
    while batch % nb:
        nb -= 1
    return nb, seq, seq


def _trunk(x, states, prm, t0):
    batch, seq, _ = x.shape
    nb, tile, chunk = _plan(batch, seq)
    tabs = _tables(seq, chunk, t0)
    depth = prm["win"].shape[0]
    x2d = x.reshape(batch * seq, D_MODEL)
    ffn_rows = min(1024, batch * seq)
    outs = []
    for l in range(depth):
        x2d, st_out = _mixer_call(x2d, states, prm, tabs, l, batch=batch, seq=seq, nb=nb, tile=tile, chunk=chunk)
        x2d = _ffn_call(x2d, prm, l, rows=ffn_rows, final=(l == depth - 1))
        outs.append(st_out)
    stacked = [jnp.stack([o[i] for o in outs]) for i in range(5)]
    s5_shape = (depth, batch, S5_GROUPS, S5_STATE)
    return (x2d.reshape(batch, seq, D_MODEL),
            (stacked[0].reshape(s5_shape), stacked[1].reshape(s5_shape), stacked[2], stacked[3], stacked[4]))


def kernel(x_prompt, x_sample, state_s5_re, state_s5_im, state_ret, state_ssd, cache_ssd_conv, norm1_g, w_in, s5_lam_re, s5_lam_im, s5_log_dt, s5_b_re, s5_b_im, s5_c_re, s5_c_im, s5_d, s5_w_glu, s5_b_glu, ssd_conv_w, ssd_conv_b, ssd_dt_bias, ssd_a_log, ssd_d, ssd_norm_g, w_out, norm2_g, w_gate, w_up, w_down, final_norm_g):
    prm = _prepare(norm1_g, w_in, s5_lam_re, s5_lam_im, s5_log_dt, s5_b_re, s5_b_im, s5_c_re, s5_c_im,
                   s5_d, s5_w_glu, s5_b_glu, ssd_conv_w, ssd_conv_b, ssd_dt_bias, ssd_a_log, ssd_d, ssd_norm_g,
                   w_out, norm2_g, w_gate, w_up, w_down, final_norm_g)
    depth, bs = state_s5_re.shape[:2]
    past_len = 1024

    y_prompt, p_states = _trunk(x_prompt, None, prm, 0)
    s_in = (state_s5_re.reshape(depth, bs, SUBLANES, LANES), state_s5_im.reshape(depth, bs, SUBLANES, LANES),
            state_ret, state_ssd, cache_ssd_conv)
    y_sample, s_states = _trunk(x_sample, s_in, prm, past_len)
    return (y_prompt, y_sample) + p_states + s_states
```

```python
import functools
import math

import jax
import jax.numpy as jnp
from jax import lax
from jax.experimental import pallas as pl
from jax.experimental.pallas import tpu as pltpu

F32 = jnp.float32
BF16 = jnp.bfloat16

D_MODEL = 1024
EPS = 1e-6
ROPE_BASE = 10000.0
S5_WIDTH = 256
S5_GROUPS = 16
S5_GROUP_CH = 16
S5_STATE = 64
S5_LANES = S5_GROUPS * S5_STATE
RET_HEADS = 4
RET_DK = 64
RET_DV = 64
RET_WIDTH = RET_HEADS * RET_DV
SSD_WIDTH = 512
SSD_HEADDIM = 64
SSD_HEADS = 8
SSD_GROUPS = 2
SSD_DSTATE = 128
SSD_CONV = 4
SSD_CONV_DIM = SSD_WIDTH + 2 * SSD_GROUPS * SSD_DSTATE
D_FF = 2816
LANES = 128
SUBLANES = 8
MXU_DEPTH = 256

U0 = 0
Q0 = U0 + S5_WIDTH
K0 = Q0 + RET_HEADS * RET_DK
V0 = K0 + RET_HEADS * RET_DK
G0 = V0 + RET_WIDTH
Z0 = G0 + RET_WIDTH
X0 = Z0 + SSD_WIDTH
DT0 = X0 + SSD_CONV_DIM
IN_PAD = DT0 + LANES

CONV_TAIL = SSD_CONV - 1
CONV_LO = SUBLANES - CONV_TAIL

VMEM_LIMIT = 60000 * 1024


def _dot(a, b):
    return jnp.dot(a, b, preferred_element_type=F32)


def _dot_nt(a, b):
    return lax.dot_general(a, b, (((1,), (1,)), ((), ())), preferred_element_type=F32)


def _sigmoid(x):
    return 1.0 / (1.0 + jnp.exp(-x))


def _silu(x):
    return x * _sigmoid(x)


def _gelu_tanh(x):
    c = math.sqrt(2.0 / math.pi)
    return x * (0.5 * (1.0 + jnp.tanh(c * (x + 0.044715 * (x * x * x)))))


def _softplus(x):
    return jnp.maximum(x, 0.0) + jnp.log1p(jnp.exp(-jnp.abs(x)))


def _rms(x, g):
    ms = jnp.mean(x * x, axis=-1, keepdims=True)
    return x * lax.rsqrt(ms + EPS) * g


NSPLIT = 2


def _split(x):
    parts, r = [], x
    for i in range(NSPLIT):
        p = r.astype(BF16)
        parts.append(p)
        if i + 1 < NSPLIT:
            r = r - p.astype(F32)
    return jnp.concatenate(parts, axis=-1)


def _unsplit(y):
    w = y.shape[-1] // NSPLIT
    out = y[:, 0:w]
    for i in range(1, NSPLIT):
        out = out + y[:, i * w:(i + 1) * w]
    return out


def _s5_param_kernel(lr_ref, li_ref, ldt_ref, br_ref, bi_ref, ar_ref, ai_ref, bbr_ref, bbi_ref):
    depth = lr_ref.shape[0]
    for l in range(depth):
        lr = lr_ref[l:l + 1, :]
        li = li_ref[l:l + 1, :]
        dt = jnp.exp(ldt_ref[l:l + 1, :])
        mag = jnp.exp(lr * dt)
        ar = mag * jnp.cos(li * dt)
        ai = mag * jnp.sin(li * dt)
        den = lr * lr + li * li
        nr = ar - 1.0
        ni = ai
        kr = (nr * lr + ni * li) / den
        ki = (ni * lr - nr * li) / den
        br = br_ref[l]
        bi = bi_ref[l]
        bbr_ref[l] = kr * br - ki * bi
        bbi_ref[l] = kr * bi + ki * br
        ar_ref[l:l + 1, :] = ar
        ai_ref[l:l + 1, :] = ai


def _s5_params(lam_re, lam_im, log_dt, b_re, b_im):
    depth = lam_re.shape[0]
    lr = lam_re.reshape(depth, S5_LANES)
    li = lam_im.reshape(depth, S5_LANES)
    ldt = jnp.repeat(log_dt, S5_STATE, axis=-1)
    brt = jnp.transpose(b_re, (0, 3, 1, 2)).reshape(depth, S5_GROUP_CH, S5_LANES)
    bit = jnp.transpose(b_im, (0, 3, 1, 2)).reshape(depth, S5_GROUP_CH, S5_LANES)
    out_shape = (jax.ShapeDtypeStruct((depth, S5_LANES), F32),
                 jax.ShapeDtypeStruct((depth, S5_LANES), F32),
                 jax.ShapeDtypeStruct((depth, S5_GROUP_CH, S5_LANES), F32),
                 jax.ShapeDtypeStruct((depth, S5_GROUP_CH, S5_LANES), F32))
    return pl.pallas_call(_s5_param_kernel, out_shape=out_shape, name="s5_params")(lr, li, ldt, brt, bit)


def _round_robin(chains, lead=None):
    chains = list(chains)
    first = True
    while chains:
        for c in list(chains):
            if next(c, StopIteration) is StopIteration:
                chains.remove(c)
        if first and lead is not None:
            for _ in lead:
                pass
        first = False


def _mixer_kernel(*refs, nb, tile, chunk, n_t, zero_init):
    (x_ref, g1_ref, win_ref, cos_ref, sin_ref,
     a8r_ref, a8i_ref, bb_ref, ct_ref, s5d_ref, wglu_ref, bglu_ref,
     dec_ref, qw_ref, kw_ref, cdc_ref, mbd_ref, mavg_ref, hmq_ref, hmv_ref, tri_ref, e3_ref,
     cw_ref, cb_ref, dtb_ref, alog_ref, dskip_ref, ng_ref, wout_ref) = refs[:29]
    refs = refs[29:]
    if not zero_init:
        s5r_in, s5i_in, ret_in, ssd_in, conv_in = refs[:5]
        refs = refs[5:]
    (xo_ref, s5r_out, s5i_out, ret_out, ssd_out, conv_out,
     proj_ref, hn_ref, mix_ref, hb_all, xp_all, ret_sc, ssd_sc, conv_sc) = refs
    unrolled = tile > chunk

    C = chunk
    nchunk = tile // C
    ic = pl.program_id(0) % n_t if n_t > 1 else 0
    half = RET_HEADS * RET_DK // 2
    hpg = SSD_HEADS // SSD_GROUPS
    gw = hpg * SSD_HEADDIM
    ret_blocks = [(h, hf, hf * half + h * (RET_DK // 2), h * RET_DV)
                  for h in range(RET_HEADS) for hf in range(2)]

    def init_states():
        if zero_init:
            s5r_out[...] = jnp.zeros_like(s5r_out)
            s5i_out[...] = jnp.zeros_like(s5i_out)
            ret_sc[...] = jnp.zeros_like(ret_sc)
            ssd_sc[...] = jnp.zeros_like(ssd_sc)
            conv_sc[...] = jnp.zeros_like(conv_sc)
            return
        s5r_out[...] = s5r_in[...]
        s5i_out[...] = s5i_in[...]
        ret_sc[...] = jnp.zeros_like(ret_sc)
        conv_sc[...] = jnp.zeros_like(conv_sc)
        for b in range(nb):
            for h, hf, r0, c0 in ret_blocks:
                ret_sc[b, r0:r0 + RET_DK // 2, c0:c0 + RET_DV] = ret_in[b, h, hf * (RET_DK // 2):(hf + 1) * (RET_DK // 2), :]
            for g in range(SSD_GROUPS):
                ssd_sc[b, g] = ssd_in[b, g * hpg:(g + 1) * hpg].reshape(gw, SSD_DSTATE).T
            conv_sc[b, CONV_LO:SUBLANES, :] = conv_in[b]

    def final_states():
        for b in range(nb):
            for h, hf, r0, c0 in ret_blocks:
                ret_out[b, h, hf * (RET_DK // 2):(hf + 1) * (RET_DK // 2), :] = ret_sc[b, r0:r0 + RET_DK // 2, c0:c0 + RET_DV]
            for g in range(SSD_GROUPS):
                ssd_out[b, g * hpg:(g + 1) * hpg] = ssd_sc[b, g].T.reshape(hpg, SSD_HEADDIM, SSD_DSTATE)
            conv_out[b] = conv_sc[b, CONV_LO:SUBLANES, :]

    if n_t > 1:
        pl.when(ic == 0)(init_states)
    else:
        init_states()

    hn_ref[...] = _rms(x_ref[...].reshape(nb * tile, D_MODEL), g1_ref[...]).astype(BF16)

    def proj_pieces(rows):
        hn = hn_ref[rows, :]
        for lo in range(0, IN_PAD, MXU_DEPTH):
            hi = min(lo + MXU_DEPTH, IN_PAD)
            proj_ref[rows, lo:hi] = _dot(hn, win_ref[:, lo:hi])
            yield

    causal = (lax.broadcasted_iota(jnp.int32, (C, C), 0) >= lax.broadcasted_iota(jnp.int32, (C, C), 1))
    lane128 = lax.broadcasted_iota(jnp.int32, (1, LANES), 1)

    def pslab(rows, lo, hi):
        return proj_ref[rows, lo:hi]

    def s5_chain(bi, ci, r0):
        rows = pl.ds(r0, C)
        hb_ref = hb_all.at[bi if unrolled else 0]
        u = pslab(rows,U0, U0 + S5_WIDTH)
        bu = _dot(u.astype(BF16), bb_ref[...])
        nslab = S5_LANES // LANES
        pitch = C + SUBLANES
        for s in range(2 * nslab):
            hb_ref[s * pitch:s * pitch + C, :] = bu[:, s * LANES:(s + 1) * LANES]
        yield
        ar = a8r_ref[...]
        ai = a8i_ref[...]
        hr = s5r_out[bi]
        hi = s5i_out[bi]
        for t in range(C):
            ld_r = pl.ds(t, nslab, stride=pitch)
            ld_i = pl.ds(nslab * pitch + t, nslab, stride=pitch)
            hr, hi = (ar * hr - ai * hi + hb_ref[ld_r, :], ar * hi + ai * hr + hb_ref[ld_i, :])
            hb_ref[ld_r, :] = hr
            hb_ref[ld_i, :] = hi
        s5r_out[bi] = hr
        s5i_out[bi] = hi
        yield
        hall = jnp.concatenate([hb_ref[s * pitch:s * pitch + C, :].astype(BF16) for s in range(2 * nslab)],
                               axis=-1)
        y = s5d_ref[...] * u + _dot_nt(hall, ct_ref[...])
        yield
        zg = _gelu_tanh(y)
        gl = _dot(zg.astype(BF16), wglu_ref[...]) + bglu_ref[...]
        yield
        mix_ref[rows, 0:S5_WIDTH] = zg * _sigmoid(gl)

    def ret_chain(bi, ci, r0):
        rows = pl.ds(r0, C)
        q1 = pslab(rows,Q0, Q0 + half)
        q2 = pslab(rows,Q0 + half, Q0 + 2 * half)
        k1 = pslab(rows,K0, K0 + half)
        k2 = pslab(rows,K0 + half, K0 + 2 * half)
        cs_ = cos_ref[pl.ds(ci * C, C), :]
        sn_ = sin_ref[pl.ds(ci * C, C), :]
        qr = jnp.concatenate([q1 * cs_ - q2 * sn_, q1 * sn_ + q2 * cs_], axis=-1)
        kr = jnp.concatenate([k1 * cs_ - k2 * sn_, k1 * sn_ + k2 * cs_], axis=-1) * (RET_DK ** -0.5)
        gate = pslab(rows,G0, G0 + RET_WIDTH)
        s_prev = ret_sc[bi]
        qb = qr.astype(BF16)
        kb = kr.astype(BF16)
        vb = pslab(rows,V0, V0 + RET_WIDTH).astype(BF16)
        o = _dot((qr * qw_ref[...]).astype(BF16), s_prev.astype(BF16))
        yield
        pairs = []
        for p in range(RET_HEADS // 2):
            k2 = jnp.concatenate([kb * hmq_ref[2 * p], kb * hmq_ref[2 * p + 1]], axis=0)
            pairs.append((_dot_nt(qb, k2) * dec_ref[p]).astype(BF16))
            yield
        kt = (kr * kw_ref[...]).T
        kv = _dot(kt.astype(BF16), vb)
        ret_sc[bi] = s_prev * cdc_ref[...] + kv * mbd_ref[...]
        yield
        for p in range(RET_HEADS // 2):
            v2 = jnp.concatenate([vb * hmv_ref[2 * p], vb * hmv_ref[2 * p + 1]], axis=0)
            o = o + _dot(pairs[p], v2)
            yield
        mavg = mavg_ref[...]
        mu = _dot(o.astype(BF16), mavg)
        yield
        dlt = o - mu
        var = _dot((dlt * dlt).astype(BF16), mavg)
        yield
        mix_ref[rows, S5_WIDTH:S5_WIDTH + RET_WIDTH] = _silu(gate) * (dlt * lax.rsqrt(var + EPS))

    def ssd_chain(bi, ci, r0):
        rows = pl.ds(r0, C)
        z = pslab(rows,Z0, Z0 + SSD_WIDTH)
        xbc = pslab(rows,X0, X0 + SSD_CONV_DIM)
        xp_ref = xp_all.at[bi if unrolled else 0]
        xp_ref[0:SUBLANES, :] = conv_sc[bi]
        xp_ref[SUBLANES:SUBLANES + C, :] = xbc
        cw = cw_ref[...]
        acc = cb_ref[...] + cw[CONV_TAIL:SSD_CONV] * xbc
        for i in range(CONV_TAIL):
            acc = acc + cw[i:i + 1] * xp_ref[CONV_LO + i:CONV_LO + i + C, :]
        conv_sc[bi] = xp_ref[C:C + SUBLANES, :]
        xc = _silu(acc)
        xs = xc[:, 0:SSD_WIDTH]
        ngl = SSD_DSTATE
        bm = xc[:, SSD_WIDTH:SSD_WIDTH + SSD_GROUPS * ngl]
        cm = xc[:, SSD_WIDTH + SSD_GROUPS * ngl:SSD_CONV_DIM]
        dt = _softplus(pslab(rows,DT0, DT0 + LANES) + dtb_ref[...])
        a_row = jnp.where(lane128 < SSD_HEADS, -jnp.exp(alog_ref[...]), 0.0)
        dta = dt * a_row
        cs_parts = _dot(tri_ref[...], _split(dta))
        yield
        cs = _unsplit(cs_parts)
        dt_e = _dot(_split(dt), e3_ref[...])
        yield
        cs_e = _dot(_split(cs), e3_ref[...])
        yield
        last_e = cs_e[C - 1:C, :]
        ecs_e = jnp.exp(cs_e)
        wend_e = jnp.exp(last_e - cs_e)
        cdec_e = jnp.exp(last_e)
        xdt = xs * dt_e
        cst = cs.T
        ys = []
        for g in range(SSD_GROUPS):
            cm_g = cm[:, g * ngl:(g + 1) * ngl].astype(BF16)
            bm_g = bm[:, g * ngl:(g + 1) * ngl]
            cbm = _dot_nt(cm_g, bm_g.astype(BF16))
            yield
            st = ssd_sc[bi, g]
            xdt_g = xdt[:, g * gw:(g + 1) * gw]
            xdt_b = xdt_g.astype(BF16)
            yg = _dot(cm_g, st.astype(BF16)) * ecs_e[:, g * gw:(g + 1) * gw]
            yield
            new = _dot(bm_g.T.astype(BF16), (xdt_g * wend_e[:, g * gw:(g + 1) * gw]).astype(BF16))
            ssd_sc[bi, g] = st * cdec_e[:, g * gw:(g + 1) * gw] + new
            yield
            for hl in range(0, hpg, 2):
                ws = []
                for h in (g * hpg + hl, g * hpg + hl + 1):
                    seg = cs[:, h:h + 1] - cst[h:h + 1, :]
                    ws.append(cbm * jnp.where(causal, jnp.exp(seg), 0.0))
                w2 = jnp.concatenate(ws, axis=-1).astype(BF16)
                x2 = jnp.concatenate([xdt_b * hmv_ref[hl], xdt_b * hmv_ref[hl + 1]], axis=0)
                yg = yg + _dot(w2, x2)
                yield
            ys.append(yg)
        yss = jnp.concatenate(ys, axis=-1) + dskip_ref[...] * xs
        mix_ref[rows, S5_WIDTH + RET_WIDTH:D_MODEL] = _rms(yss * _silu(z), ng_ref[...])

    def out_chain(b, t0, n):
        mixb = mix_ref[b * tile + t0:b * tile + t0 + n, :].astype(BF16)
        for lo in range(0, D_MODEL, D_MODEL // 2):
            cols = slice(lo, lo + D_MODEL // 2)
            xo_ref[b, t0:t0 + n, cols] = x_ref[b, t0:t0 + n, cols] + _dot(mixb, wout_ref[:, cols])
            yield

    def head_groups(bi, ci, r0):
        return [s5_chain(bi, ci, r0), ret_chain(bi, ci, r0), ssd_chain(bi, ci, r0)]

    def chain_all(gens):
        for g in gens:
            yield from g

    if unrolled:
        _round_robin([proj_pieces(pl.ds(b * tile, C)) for b in range(nb)])
        prev = []
        for ci in range(nchunk):
            ahead = None
            if ci + 1 < nchunk:
                ahead = chain_all([proj_pieces(pl.ds(b * tile + (ci + 1) * C, C)) for b in range(nb)])
            groups = [g for b in range(nb) for g in head_groups(b, ci, b * tile + ci * C)]
            _round_robin(groups + prev, ahead)
            prev = [out_chain(b, ci * C, C) for b in range(nb)]
        _round_robin(prev)
    else:
        proj_ref[...] = _dot(hn_ref[...], win_ref[...])

        def seg_body(j, carry):
            _round_robin(head_groups(j, 0, pl.multiple_of(j * C, C)))
            return carry
        lax.fori_loop(0, nb, seg_body, 0)
        res = x_ref[...].reshape(nb * tile, D_MODEL) + _dot(mix_ref[...].astype(BF16), wout_ref[...])
        xo_ref[...] = res.reshape(nb, tile, D_MODEL)

    if n_t > 1:
        pl.when(ic == n_t - 1)(final_states)
    else:
        final_states()


def _const_spec(shape, l=None):
    if l is None:
        return pl.BlockSpec(shape, lambda k: (0,) * len(shape), pipeline_mode=pl.Buffered(1))
    nd = len(shape)
    return pl.BlockSpec((None,) + tuple(shape), lambda k: (l,) + (0,) * nd, pipeline_mode=pl.Buffered(1))


def _mixer_call(x2d, states, prm, tabs, l, *, batch, seq, nb, tile, chunk):
    C = chunk
    n_t = seq // tile
    total = (batch // nb) * n_t
    rows = nb * tile
    zero_init = states is None
    x3d = x2d.reshape(batch, seq, D_MODEL)
    row_spec = pl.BlockSpec((nb, tile, D_MODEL), lambda k: (k // n_t, k % n_t, 0))
    ways = nb if tile > C else 1
    rope_spec = pl.BlockSpec((tile, LANES), lambda k: (k % n_t, 0))
    st_shapes = [(SUBLANES, LANES), (SUBLANES, LANES), (RET_HEADS, RET_DK, RET_DV),
                 (SSD_HEADS, SSD_HEADDIM, SSD_DSTATE), (CONV_TAIL, SSD_CONV_DIM)]
    st_out_specs = [pl.BlockSpec((nb,) + s, lambda k, _n=len(s): (k // n_t,) + (0,) * _n) for s in st_shapes]
    st_in_specs = [pl.BlockSpec((None, nb) + s, lambda k, _n=len(s): (l, k // n_t) + (0,) * _n)
                   for s in st_shapes]
    in_specs = [
        row_spec,
        _const_spec((1, D_MODEL), l),
        _const_spec((D_MODEL, IN_PAD), l),
        rope_spec,
        rope_spec,
        _const_spec((SUBLANES, LANES), l), _const_spec((SUBLANES, LANES), l),
        _const_spec((S5_WIDTH, 2 * S5_LANES), l), _const_spec((S5_WIDTH, 2 * S5_LANES), l),
        _const_spec((1, S5_WIDTH), l), _const_spec((S5_WIDTH, S5_WIDTH), l), _const_spec((1, S5_WIDTH), l),
        _const_spec((RET_HEADS // 2, C, 2 * C)), _const_spec((C, RET_WIDTH)), _const_spec((C, RET_WIDTH)),
        _const_spec((1, RET_WIDTH)), _const_spec((RET_WIDTH, RET_WIDTH)), _const_spec((RET_WIDTH, RET_WIDTH)),
        _const_spec((RET_HEADS, C, RET_WIDTH)), _const_spec((RET_HEADS, C, RET_WIDTH)),
        _const_spec((C, C)), _const_spec((NSPLIT * LANES, SSD_WIDTH)),
        _const_spec((SSD_CONV, SSD_CONV_DIM), l), _const_spec((1, SSD_CONV_DIM), l),
        _const_spec((1, LANES), l), _const_spec((1, LANES), l),
        _const_spec((1, SSD_WIDTH), l), _const_spec((1, SSD_WIDTH), l),
        _const_spec((D_MODEL, D_MODEL), l),
    ] + ([] if zero_init else st_in_specs)
    out_specs = [row_spec] + st_out_specs
    out_shape = ([jax.ShapeDtypeStruct(x3d.shape, F32)]
                 + [jax.ShapeDtypeStruct((batch,) + s, F32) for s in st_shapes])
    scratch = [pltpu.VMEM((rows, IN_PAD), F32), pltpu.VMEM((rows, D_MODEL), BF16),
               pltpu.VMEM((rows, D_MODEL), F32),
               pltpu.VMEM((ways, 2 * S5_LANES // LANES * (C + SUBLANES), LANES), F32),
               pltpu.VMEM((ways, C + SUBLANES, SSD_CONV_DIM), F32),
               pltpu.VMEM((nb, RET_HEADS * RET_DK, RET_WIDTH), F32),
               pltpu.VMEM((nb, SSD_GROUPS, SSD_DSTATE, SSD_WIDTH // SSD_GROUPS), F32),
               pltpu.VMEM((nb, SUBLANES, SSD_CONV_DIM), F32)]
    kern = functools.partial(_mixer_kernel, nb=nb, tile=tile, chunk=C, n_t=n_t, zero_init=zero_init)
    outs = pl.pallas_call(
        kern, grid=(total,), in_specs=in_specs, out_specs=out_specs, out_shape=out_shape,
        scratch_shapes=scratch, name=f"mixer_c{C}",
        compiler_params=pltpu.CompilerParams(dimension_semantics=("arbitrary",),
                                             vmem_limit_bytes=VMEM_LIMIT),
    )(x3d, prm["g1"], prm["win"], tabs["cos"], tabs["sin"],
      prm["a8r"], prm["a8i"], prm["bb"], prm["ct"], prm["s5d"], prm["wglu"], prm["bglu"],
      tabs["dec"], tabs["qw"], tabs["kw"], tabs["cdc"], tabs["mbd"], tabs["mavg"], tabs["hmq"], tabs["hmv"],
      tabs["tri"], tabs["e3"],
      prm["cw"], prm["cb"], prm["dtb"], prm["alog"], prm["dskip"], prm["ng"], prm["wout"],
      *(() if zero_init else states))
    return outs[0].reshape(x2d.shape), tuple(outs[1:])


def _ffn_kernel(x_ref, g2_ref, wg_ref, wu_ref, wd_ref, gf_ref, o_ref, *, final, nsub):
    split = (D_FF // (2 * MXU_DEPTH) + 1) * MXU_DEPTH
    sub = x_ref.shape[0] // nsub

    def chain(rows):
        x = x_ref[rows, :]
        hn = _rms(x, g2_ref[...]).astype(BF16)
        acc = x
        for lo, hi in ((0, split), (split, D_FF)):
            gt = _dot(hn, wg_ref[:, lo:hi])
            up = _dot(hn, wu_ref[:, lo:hi])
            yield
            acc = acc + _dot((_silu(gt) * up).astype(BF16), wd_ref[lo:hi, :])
            yield
        if final:
            acc = _rms(acc, gf_ref[...])
        o_ref[rows, :] = acc

    _round_robin([chain(pl.ds(i * sub, sub)) for i in range(nsub)])


def _ffn_call(x2d, prm, l, *, rows, final):
    n = x2d.shape[0] // rows
    row_spec = pl.BlockSpec((rows, D_MODEL), lambda i: (i, 0))

    def wspec(shape, layer=True):
        nd = len(shape)
        if layer:
            return pl.BlockSpec((None,) + shape, lambda i: (l,) + (0,) * nd, pipeline_mode=pl.Buffered(1))
        return pl.BlockSpec(shape, lambda i: (0,) * nd, pipeline_mode=pl.Buffered(1))

    return pl.pallas_call(
        functools.partial(_ffn_kernel, final=final, nsub=2), grid=(n,),
        in_specs=[row_spec, wspec((1, D_MODEL)), wspec((D_MODEL, D_FF)), wspec((D_MODEL, D_FF)),
                  wspec((D_FF, D_MODEL)), wspec((1, D_MODEL), layer=False)],
        out_specs=row_spec, out_shape=jax.ShapeDtypeStruct(x2d.shape, F32),
        name="ffn_final" if final else "ffn",
        compiler_params=pltpu.CompilerParams(dimension_semantics=("arbitrary",), vmem_limit_bytes=VMEM_LIMIT),
    )(x2d, prm["g2"], prm["wg"], prm["wu"], prm["wd"], prm["gf"])


def _reorder_w_in(w_in):
    depth = w_in.shape[0]
    qk = RET_HEADS * RET_DK

    def halves_first(w):
        w = w.reshape(depth, D_MODEL, RET_HEADS, 2, RET_DK // 2)
        return w.transpose(0, 1, 3, 2, 4).reshape(depth, D_MODEL, qk)

    wb = w_in.astype(BF16)
    parts = [wb[:, :, :S5_WIDTH], halves_first(wb[:, :, S5_WIDTH:S5_WIDTH + qk]),
             halves_first(wb[:, :, S5_WIDTH + qk:S5_WIDTH + 2 * qk]), wb[:, :, S5_WIDTH + 2 * qk:]]
    used = sum(p.shape[2] for p in parts)
    parts.append(jnp.zeros((depth, D_MODEL, IN_PAD - used), BF16))
    return jnp.concatenate(parts, axis=2)


def _prepare(norm1_g, w_in, s5_lam_re, s5_lam_im, s5_log_dt, s5_b_re, s5_b_im, s5_c_re, s5_c_im,
             s5_d, s5_w_glu, s5_b_glu, ssd_conv_w, ssd_conv_b, ssd_dt_bias, ssd_a_log, ssd_d, ssd_norm_g,
             w_out, norm2_g, w_gate, w_up, w_down, final_norm_g):
    depth = w_in.shape[0]
    a_re, a_im, bbr, bbi = _s5_params(s5_lam_re, s5_lam_im, s5_log_dt, s5_b_re, s5_b_im)
    rg = jnp.arange(S5_WIDTH)[:, None] // S5_GROUP_CH
    cg = jnp.arange(S5_LANES)[None, :] // S5_STATE
    blk = (rg == cg)[None]

    def bdiag_b(t):
        return jnp.where(blk, jnp.tile(t, (1, S5_GROUPS, 1)), 0.0)

    def bdiag_c(c):
        c2 = c.reshape(depth, S5_WIDTH, S5_STATE)
        return jnp.where(blk, jnp.tile(c2, (1, 1, S5_GROUPS)), 0.0)

    pad_l = ((0, 0), (0, 0), (0, LANES - SSD_HEADS))
    return {
        "g1": norm1_g[:, None, :], "win": _reorder_w_in(w_in),
        "a8r": a_re.reshape(depth, SUBLANES, LANES), "a8i": a_im.reshape(depth, SUBLANES, LANES),
        "bb": jnp.concatenate([bdiag_b(bbr), bdiag_b(bbi)], axis=-1).astype(BF16),
        "ct": jnp.concatenate([bdiag_c(s5_c_re), -bdiag_c(s5_c_im)], axis=-1).astype(BF16),
        "s5d": s5_d[:, None, :], "wglu": s5_w_glu.astype(BF16), "bglu": s5_b_glu[:, None, :],
        "cw": ssd_conv_w, "cb": ssd_conv_b[:, None, :],
        "dtb": jnp.pad(ssd_dt_bias[:, None, :], pad_l), "alog": jnp.pad(ssd_a_log[:, None, :], pad_l),
        "dskip": jnp.repeat(ssd_d, SSD_HEADDIM, axis=-1)[:, None, :], "ng": ssd_norm_g[:, None, :],
        "wout": w_out.astype(BF16), "g2": norm2_g[:, None, :],
        "wg": w_gate.astype(BF16), "wu": w_up.astype(BF16), "wd": w_down.astype(BF16),
        "gf": final_norm_g[None, :],
    }


def _tables(seq, chunk, t0):
    C = chunk
    lg = [math.log(1.0 - 2.0 ** (-5.0 - h)) for h in range(RET_HEADS)]
    lgv = jnp.asarray(lg, F32)
    i = jnp.arange(C, dtype=F32)
    rel = i[:, None] - i[None, :]
    dec = jnp.where(rel >= 0, jnp.exp(lgv[:, None, None] * jnp.maximum(rel, 0.0)), 0.0)
    dec = jnp.concatenate([dec[0::2], dec[1::2]], axis=-1)
    lane = jnp.arange(RET_WIDTH)
    hq = (lane % (RET_WIDTH // 2)) // (RET_DK // 2)
    hv = lane // RET_DV
    qw = jnp.exp(lgv[hq][None, :] * (i + 1.0)[:, None])
    kw = jnp.exp(lgv[hq][None, :] * (C - 1.0 - i)[:, None])
    cdc = jnp.exp(lgv[hv] * C)[None, :]
    mbd = (hq[:, None] == hv[None, :]).astype(F32)
    mavg = jnp.where(hv[:, None] == hv[None, :], 1.0 / RET_DV, 0.0).astype(BF16)
    heads = jnp.arange(RET_HEADS)[:, None, None]
    hmq = jnp.broadcast_to(hq[None, None, :] == heads, (RET_HEADS, C, RET_WIDTH)).astype(BF16)
    hmv = jnp.broadcast_to(hv[None, None, :] == heads, (RET_HEADS, C, RET_WIDTH)).astype(BF16)
    tri = (i[:, None] >= i[None, :]).astype(BF16)
    r = jnp.arange(NSPLIT * LANES) % LANES
    e3 = (r[:, None] == (jnp.arange(SSD_WIDTH) // SSD_HEADDIM)[None, :]).astype(BF16)
    half = RET_DK // 2
    inv_freq = ROPE_BASE ** (-jnp.arange(half, dtype=F32) / half)
    pos = (t0 + jnp.arange(seq)).astype(F32)
    ang = pos[:, None] * jnp.tile(inv_freq, RET_HEADS)[None, :]
    return {"dec": dec, "qw": qw, "kw": kw, "cdc": cdc, "mbd": mbd, "mavg": mavg, "hmq": hmq, "hmv": hmv,
            "tri": tri, "e3": e3, "cos": jnp.cos(ang), "sin": jnp.sin(ang)}


def _plan(batch, seq):
    if seq >= 512:
        return 1, 512, 128
    nb = max(1, min(batch, 256 // seq))
    while batch % nb:
        nb -= 1
    return nb, seq, seq


def _trunk(x, states, prm, t0):
    batch, seq, _ = x.shape
    nb, tile, chunk = _plan(batch, seq)
    tabs = _tables(seq, chunk, t0)
    depth = prm["win"].shape[0]
    x2d = x.reshape(batch * seq, D_MODEL)
    ffn_rows = min(1024, batch * seq)
    outs = []
    for l in range(depth):
        x2d, st_out = _mixer_call(x2d, states, prm, tabs, l, batch=batch, seq=seq, nb=nb, tile=tile, chunk=chunk)
        x2d = _ffn_call(x2d, prm, l, rows=ffn_rows, final=(l == depth - 1))
        outs.append(st_out)
    stacked = [jnp.stack([o[i] for o in outs]) for i in range(5)]
    s5_shape = (depth, batch, S5_GROUPS, S5_STATE)
    return (x2d.reshape(batch, seq, D_MODEL),
            (stacked[0].reshape(s5_shape), stacked[1].reshape(s5_shape), stacked[2], stacked[3], stacked[4]))


def kernel(x_prompt, x_sample, state_s5_re, state_s5_im, state_ret, state_ssd, cache_ssd_conv, norm1_g, w_in, s5_lam_re, s5_lam_im, s5_log_dt, s5_b_re, s5_b_im, s5_c_re, s5_c_im, s5_d, s5_w_glu, s5_b_glu, ssd_conv_w, ssd_conv_b, ssd_dt_bias, ssd_a_log, ssd_d, ssd_norm_g, w_out, norm2_g, w_gate, w_up, w_down, final_norm_g):
    prm = _prepare(norm1_g, w_in, s5_lam_re, s5_lam_im, s5_log_dt, s5_b_re, s5_b_im, s5_c_re, s5_c_im,
                   s5_d, s5_w_glu, s5_b_glu, ssd_conv_w, ssd_conv_b, ssd_dt_bias, ssd_a_log, ssd_d, ssd_norm_g,
                   w_out, norm2_g, w_gate, w_up, w_down, final_norm_g)
    depth, bs = state_s5_re.shape[:2]
    past_len = 1024

    y_prompt, p_states = _trunk(x_prompt, None, prm, 0)
    s_in = (state_s5_re.reshape(depth, bs, SUBLANES, LANES), state_s5_im.reshape(depth, bs, SUBLANES, LANES),
            state_ret, state_ssd, cache_ssd_conv)
    y_sample, s_states = _trunk(x_sample, s_in, prm, past_len)
    return (y_prompt, y_sample) + p_states + s_states
```

```python
import functools
import math

import jax
import jax.numpy as jnp
from jax import lax
from jax.experimental import pallas as pl
from jax.experimental.pallas import tpu as pltpu

F32 = jnp.float32
BF16 = jnp.bfloat16

D_MODEL = 1024
EPS = 1e-6
ROPE_BASE = 10000.0
S5_WIDTH = 256
S5_GROUPS = 16
S5_GROUP_CH = 16
S5_STATE = 64
S5_LANES = S5_GROUPS * S5_STATE
RET_HEADS = 4
RET_DK = 64
RET_DV = 64
RET_WIDTH = RET_HEADS * RET_DV
SSD_WIDTH = 512
SSD_HEADDIM = 64
SSD_HEADS = 8
SSD_GROUPS = 2
SSD_DSTATE = 128
SSD_CONV = 4
SSD_CONV_DIM = SSD_WIDTH + 2 * SSD_GROUPS * SSD_DSTATE
D_FF = 2816
LANES = 128
SUBLANES = 8
MXU_DEPTH = 256

U0 = 0
Q0 = U0 + S5_WIDTH
K0 = Q0 + RET_HEADS * RET_DK
V0 = K0 + RET_HEADS * RET_DK
G0 = V0 + RET_WIDTH
Z0 = G0 + RET_WIDTH
X0 = Z0 + SSD_WIDTH
DT0 = X0 + SSD_CONV_DIM
IN_PAD = DT0 + LANES

CONV_TAIL = SSD_CONV - 1
CONV_LO = SUBLANES - CONV_TAIL

VMEM_LIMIT = 62 * 1024 * 1024


def _dot(a, b):
    return jnp.dot(a, b, preferred_element_type=F32)


def _dot_nt(a, b):
    return lax.dot_general(a, b, (((1,), (1,)), ((), ())), preferred_element_type=F32)


def _sigmoid(x):
    return 1.0 / (1.0 + jnp.exp(-x))


def _silu(x):
    return x * _sigmoid(x)


def _gelu_tanh(x):
    c = math.sqrt(2.0 / math.pi)
    return x * (0.5 * (1.0 + jnp.tanh(c * (x + 0.044715 * (x * x * x)))))


def _softplus(x):
    return jnp.maximum(x, 0.0) + jnp.log1p(jnp.exp(-jnp.abs(x)))


def _rms(x, g):
    ms = jnp.mean(x * x, axis=-1, keepdims=True)
    return x * lax.rsqrt(ms + EPS) * g


NSPLIT = 2


def _split(x):
    parts, r = [], x
    for i in range(NSPLIT):
        p = r.astype(BF16)
        parts.append(p)
        if i + 1 < NSPLIT:
            r = r - p.astype(F32)
    return jnp.concatenate(parts, axis=-1)


def _unsplit(y):
    w = y.shape[-1] // NSPLIT
    out = y[:, 0:w]
    for i in range(1, NSPLIT):
        out = out + y[:, i * w:(i + 1) * w]
    return out


def _s5_param_kernel(lr_ref, li_ref, ldt_ref, br_ref, bi_ref, ar_ref, ai_ref, bbr_ref, bbi_ref):
    depth = lr_ref.shape[0]
    for l in range(depth):
        lr = lr_ref[l:l + 1, :]
        li = li_ref[l:l + 1, :]
        dt = jnp.exp(ldt_ref[l:l + 1, :])
        mag = jnp.exp(lr * dt)
        ar = mag * jnp.cos(li * dt)
        ai = mag * jnp.sin(li * dt)
        den = lr * lr + li * li
        nr = ar - 1.0
        ni = ai
        kr = (nr * lr + ni * li) / den
        ki = (ni * lr - nr * li) / den
        br = br_ref[l]
        bi = bi_ref[l]
        bbr_ref[l] = kr * br - ki * bi
        bbi_ref[l] = kr * bi + ki * br
        ar_ref[l:l + 1, :] = ar
        ai_ref[l:l + 1, :] = ai


def _s5_params(lam_re, lam_im, log_dt, b_re, b_im):
    depth = lam_re.shape[0]
    lr = lam_re.reshape(depth, S5_LANES)
    li = lam_im.reshape(depth, S5_LANES)
    ldt = jnp.repeat(log_dt, S5_STATE, axis=-1)
    brt = jnp.transpose(b_re, (0, 3, 1, 2)).reshape(depth, S5_GROUP_CH, S5_LANES)
    bit = jnp.transpose(b_im, (0, 3, 1, 2)).reshape(depth, S5_GROUP_CH, S5_LANES)
    out_shape = (jax.ShapeDtypeStruct((depth, S5_LANES), F32),
                 jax.ShapeDtypeStruct((depth, S5_LANES), F32),
                 jax.ShapeDtypeStruct((depth, S5_GROUP_CH, S5_LANES), F32),
                 jax.ShapeDtypeStruct((depth, S5_GROUP_CH, S5_LANES), F32))
    return pl.pallas_call(_s5_param_kernel, out_shape=out_shape, name="s5_params")(lr, li, ldt, brt, bit)


def _round_robin(chains, lead=None):
    chains = list(chains)
    first = True
    while chains:
        for c in list(chains):
            if next(c, StopIteration) is StopIteration:
                chains.remove(c)
        if first and lead is not None:
            for _ in lead:
                pass
        first = False


def _mixer_kernel(*refs, nb, tile, chunk, n_t, zero_init):
    (x_ref, g1_ref, win_ref, cos_ref, sin_ref,
     a8r_ref, a8i_ref, bb_ref, ct_ref, s5d_ref, wglu_ref, bglu_ref,
     dec_ref, qw_ref, kw_ref, cdc_ref, mbd_ref, mavg_ref, hmq_ref, hmv_ref, tri_ref,
     cw_ref, cb_ref, dtb_ref, alog_ref, dskip_ref, ng_ref, wout_ref) = refs[:28]
    refs = refs[28:]
    if not zero_init:
        s5r_in, s5i_in, ret_in, ssd_in, conv_in = refs[:5]
        refs = refs[5:]
    (xo_ref, s5r_out, s5i_out, ret_out, ssd_out, conv_out,
     proj_ref, hn_ref, mix_ref, hb_all, xp_all, ret_sc, ssd_sc, conv_sc) = refs
    unrolled = tile > chunk

    C = chunk
    nchunk = tile // C
    ic = pl.program_id(0) % n_t if n_t > 1 else 0
    half = RET_HEADS * RET_DK // 2
    hpg = SSD_HEADS // SSD_GROUPS
    gw = hpg * SSD_HEADDIM
    ret_blocks = [(h, hf, hf * half + h * (RET_DK // 2), h * RET_DV)
                  for h in range(RET_HEADS) for hf in range(2)]

    def init_states():
        if zero_init:
            s5r_out[...] = jnp.zeros_like(s5r_out)
            s5i_out[...] = jnp.zeros_like(s5i_out)
            ret_sc[...] = jnp.zeros_like(ret_sc)
            ssd_sc[...] = jnp.zeros_like(ssd_sc)
            conv_sc[...] = jnp.zeros_like(conv_sc)
            return
        s5r_out[...] = s5r_in[...]
        s5i_out[...] = s5i_in[...]
        ret_sc[...] = jnp.zeros_like(ret_sc)
        conv_sc[...] = jnp.zeros_like(conv_sc)
        for b in range(nb):
            for h, hf, r0, c0 in ret_blocks:
                ret_sc[b, r0:r0 + RET_DK // 2, c0:c0 + RET_DV] = ret_in[b, h, hf * (RET_DK // 2):(hf + 1) * (RET_DK // 2), :]
            for g in range(SSD_GROUPS):
                ssd_sc[b, g] = ssd_in[b, g * hpg:(g + 1) * hpg].reshape(gw, SSD_DSTATE).T
            conv_sc[b, CONV_LO:SUBLANES, :] = conv_in[b]

    def final_states():
        for b in range(nb):
            for h, hf, r0, c0 in ret_blocks:
                ret_out[b, h, hf * (RET_DK // 2):(hf + 1) * (RET_DK // 2), :] = ret_sc[b, r0:r0 + RET_DK // 2, c0:c0 + RET_DV]
            for g in range(SSD_GROUPS):
                ssd_out[b, g * hpg:(g + 1) * hpg] = ssd_sc[b, g].T.reshape(hpg, SSD_HEADDIM, SSD_DSTATE)
            conv_out[b] = conv_sc[b, CONV_LO:SUBLANES, :]

    if n_t > 1:
        pl.when(ic == 0)(init_states)
    else:
        init_states()

    hn_ref[...] = _rms(x_ref[...].reshape(nb * tile, D_MODEL), g1_ref[...]).astype(BF16)

    def proj_pieces(rows):
        hn = hn_ref[rows, :]
        for lo in range(0, IN_PAD, MXU_DEPTH):
            hi = min(lo + MXU_DEPTH, IN_PAD)
            proj_ref[rows, lo:hi] = _dot(hn, win_ref[:, lo:hi])
            yield

    causal = (lax.broadcasted_iota(jnp.int32, (C, C), 0) >= lax.broadcasted_iota(jnp.int32, (C, C), 1))
    lane128 = lax.broadcasted_iota(jnp.int32, (1, LANES), 1)

    def pslab(rows, lo, hi):
        return proj_ref[rows, lo:hi]

    first_head = lane128 < SSD_HEADDIM

    def head_lanes(c):
        n = c.shape[0]
        slabs = []
        for j in range(SSD_HEADS // 2):
            a = jnp.broadcast_to(c[:, 2 * j:2 * j + 1], (n, LANES))
            b = jnp.broadcast_to(c[:, 2 * j + 1:2 * j + 2], (n, LANES))
            slabs.append(jnp.where(first_head, a, b))
        return jnp.concatenate(slabs, axis=-1)

    def s5_chain(bi, ci, r0):
        rows = pl.ds(r0, C)
        hb_ref = hb_all.at[bi if unrolled else 0]
        u = pslab(rows,U0, U0 + S5_WIDTH)
        bu = _dot(u.astype(BF16), bb_ref[...])
        nslab = S5_LANES // LANES
        pitch = C + SUBLANES
        for s in range(2 * nslab):
            hb_ref[s * pitch:s * pitch + C, :] = bu[:, s * LANES:(s + 1) * LANES]
        yield
        ar = a8r_ref[...]
        ai = a8i_ref[...]
        hr = s5r_out[bi]
        hi = s5i_out[bi]
        for t in range(C):
            ld_r = pl.ds(t, nslab, stride=pitch)
            ld_i = pl.ds(nslab * pitch + t, nslab, stride=pitch)
            hr, hi = (ar * hr - ai * hi + hb_ref[ld_r, :], ar * hi + ai * hr + hb_ref[ld_i, :])
            hb_ref[ld_r, :] = hr
            hb_ref[ld_i, :] = hi
        s5r_out[bi] = hr
        s5i_out[bi] = hi
        yield
        hall = jnp.concatenate([hb_ref[s * pitch:s * pitch + C, :].astype(BF16) for s in range(2 * nslab)],
                               axis=-1)
        y = s5d_ref[...] * u + _dot_nt(hall, ct_ref[...])
        yield
        zg = _gelu_tanh(y)
        gl = _dot(zg.astype(BF16), wglu_ref[...]) + bglu_ref[...]
        yield
        mix_ref[rows, 0:S5_WIDTH] = zg * _sigmoid(gl)

    def ret_chain(bi, ci, r0):
        rows = pl.ds(r0, C)
        q1 = pslab(rows,Q0, Q0 + half)
        q2 = pslab(rows,Q0 + half, Q0 + 2 * half)
        k1 = pslab(rows,K0, K0 + half)
        k2 = pslab(rows,K0 + half, K0 + 2 * half)
        cs_ = cos_ref[pl.ds(ci * C, C), :]
        sn_ = sin_ref[pl.ds(ci * C, C), :]
        qr = jnp.concatenate([q1 * cs_ - q2 * sn_, q1 * sn_ + q2 * cs_], axis=-1)
        kr = jnp.concatenate([k1 * cs_ - k2 * sn_, k1 * sn_ + k2 * cs_], axis=-1) * (RET_DK ** -0.5)
        gate = pslab(rows,G0, G0 + RET_WIDTH)
        s_prev = ret_sc[bi]
        qb = qr.astype(BF16)
        kb = kr.astype(BF16)
        vb = pslab(rows,V0, V0 + RET_WIDTH).astype(BF16)
        o = _dot((qr * qw_ref[...]).astype(BF16), s_prev.astype(BF16))
        yield
        scs = []
        for h in range(RET_HEADS):
            scs.append(_dot_nt(qb * hmq_ref[h], kb) * dec_ref[h])
            yield
        kt = (kr * kw_ref[...]).T
        kv = _dot(kt.astype(BF16), vb)
        ret_sc[bi] = s_prev * cdc_ref[...] + kv * mbd_ref[...]
        yield
        for h in range(RET_HEADS):
            o = o + _dot(scs[h].astype(BF16), vb * hmv_ref[h])
            yield
        mavg = mavg_ref[...]
        mu = _dot(o.astype(BF16), mavg)
        yield
        dlt = o - mu
        var = _dot((dlt * dlt).astype(BF16), mavg)
        yield
        mix_ref[rows, S5_WIDTH:S5_WIDTH + RET_WIDTH] = _silu(gate) * (dlt * lax.rsqrt(var + EPS))

    def ssd_chain(bi, ci, r0):
        rows = pl.ds(r0, C)
        z = pslab(rows,Z0, Z0 + SSD_WIDTH)
        xbc = pslab(rows,X0, X0 + SSD_CONV_DIM)
        xp_ref = xp_all.at[bi if unrolled else 0]
        xp_ref[0:SUBLANES, :] = conv_sc[bi]
        xp_ref[SUBLANES:SUBLANES + C, :] = xbc
        cw = cw_ref[...]
        acc = cb_ref[...] + cw[CONV_TAIL:SSD_CONV] * xbc
        for i in range(CONV_TAIL):
            acc = acc + cw[i:i + 1] * xp_ref[CONV_LO + i:CONV_LO + i + C, :]
        conv_sc[bi] = xp_ref[C:C + SUBLANES, :]
        xc = _silu(acc)
        xs = xc[:, 0:SSD_WIDTH]
        ngl = SSD_DSTATE
        bm = xc[:, SSD_WIDTH:SSD_WIDTH + SSD_GROUPS * ngl]
        cm = xc[:, SSD_WIDTH + SSD_GROUPS * ngl:SSD_CONV_DIM]
        dt = _softplus(pslab(rows,DT0, DT0 + LANES) + dtb_ref[...])
        a_row = jnp.where(lane128 < SSD_HEADS, -jnp.exp(alog_ref[...]), 0.0)
        dta = dt * a_row
        cs_parts = _dot(tri_ref[...], _split(dta))
        yield
        cs = _unsplit(cs_parts)
        last = cs[C - 1:C, :]
        dt_e = head_lanes(dt)
        ecs_e = head_lanes(jnp.exp(cs))
        wend_e = head_lanes(jnp.exp(last - cs))
        cdec_e = head_lanes(jnp.exp(last))
        xdt = xs * dt_e
        cst = cs.T
        ys = []
        for g in range(SSD_GROUPS):
            cm_g = cm[:, g * ngl:(g + 1) * ngl].astype(BF16)
            bm_g = bm[:, g * ngl:(g + 1) * ngl]
            cbm = _dot_nt(cm_g, bm_g.astype(BF16))
            yield
            st = ssd_sc[bi, g]
            xdt_g = xdt[:, g * gw:(g + 1) * gw]
            xdt_b = xdt_g.astype(BF16)
            yg = _dot(cm_g, st.astype(BF16)) * ecs_e[:, g * gw:(g + 1) * gw]
            yield
            new = _dot(bm_g.T.astype(BF16), (xdt_g * wend_e[:, g * gw:(g + 1) * gw]).astype(BF16))
            ssd_sc[bi, g] = st * cdec_e[:, g * gw:(g + 1) * gw] + new
            yield
            for hl in range(hpg):
                h = g * hpg + hl
                seg = cs[:, h:h + 1] - cst[h:h + 1, :]
                lm = jnp.where(causal, jnp.exp(seg), 0.0)
                yg = yg + _dot((cbm * lm).astype(BF16), xdt_b * hmv_ref[hl])
                yield
            ys.append(yg)
        yss = jnp.concatenate(ys, axis=-1) + dskip_ref[...] * xs
        mix_ref[rows, S5_WIDTH + RET_WIDTH:D_MODEL] = _rms(yss * _silu(z), ng_ref[...])

    def out_chain(b, t0, n):
        mixb = mix_ref[b * tile + t0:b * tile + t0 + n, :].astype(BF16)
        for lo in range(0, D_MODEL, D_MODEL // 2):
            cols = slice(lo, lo + D_MODEL // 2)
            xo_ref[b, t0:t0 + n, cols] = x_ref[b, t0:t0 + n, cols] + _dot(mixb, wout_ref[:, cols])
            yield

    def head_groups(bi, ci, r0):
        return [s5_chain(bi, ci, r0), ret_chain(bi, ci, r0), ssd_chain(bi, ci, r0)]

    def chain_all(gens):
        for g in gens:
            yield from g

    if unrolled:
        _round_robin([proj_pieces(pl.ds(b * tile, C)) for b in range(nb)])
        prev = []
        for ci in range(nchunk):
            ahead = None
            if ci + 1 < nchunk:
                ahead = chain_all([proj_pieces(pl.ds(b * tile + (ci + 1) * C, C)) for b in range(nb)])
            groups = [g for b in range(nb) for g in head_groups(b, ci, b * tile + ci * C)]
            _round_robin(groups + prev, ahead)
            prev = [out_chain(b, ci * C, C) for b in range(nb)]
        _round_robin(prev)
    else:
        proj_ref[...] = _dot(hn_ref[...], win_ref[...])

        def seg_body(j, carry):
            _round_robin(head_groups(j, 0, pl.multiple_of(j * C, C)))
            return carry
        lax.fori_loop(0, nb, seg_body, 0)
        res = x_ref[...].reshape(nb * tile, D_MODEL) + _dot(mix_ref[...].astype(BF16), wout_ref[...])
        xo_ref[...] = res.reshape(nb, tile, D_MODEL)

    if n_t > 1:
        pl.when(ic == n_t - 1)(final_states)
    else:
        final_states()


def _const_spec(shape, l=None):
    if l is None:
        return pl.BlockSpec(shape, lambda k: (0,) * len(shape), pipeline_mode=pl.Buffered(1))
    nd = len(shape)
    return pl.BlockSpec((None,) + tuple(shape), lambda k: (l,) + (0,) * nd, pipeline_mode=pl.Buffered(1))


def _mixer_call(x2d, states, prm, tabs, l, *, batch, seq, nb, tile, chunk):
    C = chunk
    n_t = seq // tile
    total = (batch // nb) * n_t
    rows = nb * tile
    zero_init = states is None
    x3d = x2d.reshape(batch, seq, D_MODEL)
    row_spec = pl.BlockSpec((nb, tile, D_MODEL), lambda k: (k // n_t, k % n_t, 0))
    ways = nb if tile > C else 1
    rope_spec = pl.BlockSpec((tile, LANES), lambda k: (k % n_t, 0))
    st_shapes = [(SUBLANES, LANES), (SUBLANES, LANES), (RET_HEADS, RET_DK, RET_DV),
                 (SSD_HEADS, SSD_HEADDIM, SSD_DSTATE), (CONV_TAIL, SSD_CONV_DIM)]
    st_out_specs = [pl.BlockSpec((nb,) + s, lambda k, _n=len(s): (k // n_t,) + (0,) * _n) for s in st_shapes]
    st_in_specs = [pl.BlockSpec((None, nb) + s, lambda k, _n=len(s): (l, k // n_t) + (0,) * _n)
                   for s in st_shapes]
    in_specs = [
        row_spec,
        _const_spec((1, D_MODEL), l),
        _const_spec((D_MODEL, IN_PAD), l),
        rope_spec,
        rope_spec,
        _const_spec((SUBLANES, LANES), l), _const_spec((SUBLANES, LANES), l),
        _const_spec((S5_WIDTH, 2 * S5_LANES), l), _const_spec((S5_WIDTH, 2 * S5_LANES), l),
        _const_spec((1, S5_WIDTH), l), _const_spec((S5_WIDTH, S5_WIDTH), l), _const_spec((1, S5_WIDTH), l),
        _const_spec((RET_HEADS, C, C)), _const_spec((C, RET_WIDTH)), _const_spec((C, RET_WIDTH)),
        _const_spec((1, RET_WIDTH)), _const_spec((RET_WIDTH, RET_WIDTH)), _const_spec((RET_WIDTH, RET_WIDTH)),
        _const_spec((RET_HEADS, C, RET_WIDTH)), _const_spec((RET_HEADS, C, RET_WIDTH)),
        _const_spec((C, C)),
        _const_spec((SSD_CONV, SSD_CONV_DIM), l), _const_spec((1, SSD_CONV_DIM), l),
        _const_spec((1, LANES), l), _const_spec((1, LANES), l),
        _const_spec((1, SSD_WIDTH), l), _const_spec((1, SSD_WIDTH), l),
        _const_spec((D_MODEL, D_MODEL), l),
    ] + ([] if zero_init else st_in_specs)
    out_specs = [row_spec] + st_out_specs
    out_shape = ([jax.ShapeDtypeStruct(x3d.shape, F32)]
                 + [jax.ShapeDtypeStruct((batch,) + s, F32) for s in st_shapes])
    scratch = [pltpu.VMEM((rows, IN_PAD), F32), pltpu.VMEM((rows, D_MODEL), BF16),
               pltpu.VMEM((rows, D_MODEL), F32),
               pltpu.VMEM((ways, 2 * S5_LANES // LANES * (C + SUBLANES), LANES), F32),
               pltpu.VMEM((ways, C + SUBLANES, SSD_CONV_DIM), F32),
               pltpu.VMEM((nb, RET_HEADS * RET_DK, RET_WIDTH), F32),
               pltpu.VMEM((nb, SSD_GROUPS, SSD_DSTATE, SSD_WIDTH // SSD_GROUPS), F32),
               pltpu.VMEM((nb, SUBLANES, SSD_CONV_DIM), F32)]
    kern = functools.partial(_mixer_kernel, nb=nb, tile=tile, chunk=C, n_t=n_t, zero_init=zero_init)
    outs = pl.pallas_call(
        kern, grid=(total,), in_specs=in_specs, out_specs=out_specs, out_shape=out_shape,
        scratch_shapes=scratch, name=f"mixer_c{C}",
        compiler_params=pltpu.CompilerParams(dimension_semantics=("arbitrary",),
                                             vmem_limit_bytes=VMEM_LIMIT),
    )(x3d, prm["g1"], prm["win"], tabs["cos"], tabs["sin"],
      prm["a8r"], prm["a8i"], prm["bb"], prm["ct"], prm["s5d"], prm["wglu"], prm["bglu"],
      tabs["dec"], tabs["qw"], tabs["kw"], tabs["cdc"], tabs["mbd"], tabs["mavg"], tabs["hmq"], tabs["hmv"],
      tabs["tri"],
      prm["cw"], prm["cb"], prm["dtb"], prm["alog"], prm["dskip"], prm["ng"], prm["wout"],
      *(() if zero_init else states))
    return outs[0].reshape(x2d.shape), tuple(outs[1:])


def _ffn_kernel(x_ref, g2_ref, wg_ref, wu_ref, wd_ref, gf_ref, o_ref, *, final, nsub):
    split = (D_FF // (2 * MXU_DEPTH) + 1) * MXU_DEPTH
    sub = x_ref.shape[0] // nsub

    def chain(rows):
        x = x_ref[rows, :]
        hn = _rms(x, g2_ref[...]).astype(BF16)
        acc = x
        for lo, hi in ((0, split), (split, D_FF)):
            gt = _dot(hn, wg_ref[:, lo:hi])
            up = _dot(hn, wu_ref[:, lo:hi])
            yield
            acc = acc + _dot((_silu(gt) * up).astype(BF16), wd_ref[lo:hi, :])
            yield
        if final:
            acc = _rms(acc, gf_ref[...])
        o_ref[rows, :] = acc

    _round_robin([chain(pl.ds(i * sub, sub)) for i in range(nsub)])


def _ffn_call(x2d, prm, l, *, rows, final):
    n = x2d.shape[0] // rows
    row_spec = pl.BlockSpec((rows, D_MODEL), lambda i: (i, 0))

    def wspec(shape, layer=True):
        nd = len(shape)
        if layer:
            return pl.BlockSpec((None,) + shape, lambda i: (l,) + (0,) * nd, pipeline_mode=pl.Buffered(1))
        return pl.BlockSpec(shape, lambda i: (0,) * nd, pipeline_mode=pl.Buffered(1))

    return pl.pallas_call(
        functools.partial(_ffn_kernel, final=final, nsub=2), grid=(n,),
        in_specs=[row_spec, wspec((1, D_MODEL)), wspec((D_MODEL, D_FF)), wspec((D_MODEL, D_FF)),
                  wspec((D_FF, D_MODEL)), wspec((1, D_MODEL), layer=False)],
        out_specs=row_spec, out_shape=jax.ShapeDtypeStruct(x2d.shape, F32),
        name="ffn_final" if final else "ffn",
        compiler_params=pltpu.CompilerParams(dimension_semantics=("arbitrary",), vmem_limit_bytes=VMEM_LIMIT),
    )(x2d, prm["g2"], prm["wg"], prm["wu"], prm["wd"], prm["gf"])


def _reorder_w_in(w_in):
    depth = w_in.shape[0]
    qk = RET_HEADS * RET_DK

    def halves_first(w):
        w = w.reshape(depth, D_MODEL, RET_HEADS, 2, RET_DK // 2)
        return w.transpose(0, 1, 3, 2, 4).reshape(depth, D_MODEL, qk)

    wb = w_in.astype(BF16)
    parts = [wb[:, :, :S5_WIDTH], halves_first(wb[:, :, S5_WIDTH:S5_WIDTH + qk]),
             halves_first(wb[:, :, S5_WIDTH + qk:S5_WIDTH + 2 * qk]), wb[:, :, S5_WIDTH + 2 * qk:]]
    used = sum(p.shape[2] for p in parts)
    parts.append(jnp.zeros((depth, D_MODEL, IN_PAD - used), BF16))
    return jnp.concatenate(parts, axis=2)


def _prepare(norm1_g, w_in, s5_lam_re, s5_lam_im, s5_log_dt, s5_b_re, s5_b_im, s5_c_re, s5_c_im,
             s5_d, s5_w_glu, s5_b_glu, ssd_conv_w, ssd_conv_b, ssd_dt_bias, ssd_a_log, ssd_d, ssd_norm_g,
             w_out, norm2_g, w_gate, w_up, w_down, final_norm_g):
    depth = w_in.shape[0]
    a_re, a_im, bbr, bbi = _s5_params(s5_lam_re, s5_lam_im, s5_log_dt, s5_b_re, s5_b_im)
    rg = jnp.arange(S5_WIDTH)[:, None] // S5_GROUP_CH
    cg = jnp.arange(S5_LANES)[None, :] // S5_STATE
    blk = (rg == cg)[None]

    def bdiag_b(t):
        return jnp.where(blk, jnp.tile(t, (1, S5_GROUPS, 1)), 0.0)

    def bdiag_c(c):
        c2 = c.reshape(depth, S5_WIDTH, S5_STATE)
        return jnp.where(blk, jnp.tile(c2, (1, 1, S5_GROUPS)), 0.0)

    pad_l = ((0, 0), (0, 0), (0, LANES - SSD_HEADS))
    return {
        "g1": norm1_g[:, None, :], "win": _reorder_w_in(w_in),
        "a8r": a_re.reshape(depth, SUBLANES, LANES), "a8i": a_im.reshape(depth, SUBLANES, LANES),
        "bb": jnp.concatenate([bdiag_b(bbr), bdiag_b(bbi)], axis=-1).astype(BF16),
        "ct": jnp.concatenate([bdiag_c(s5_c_re), -bdiag_c(s5_c_im)], axis=-1).astype(BF16),
        "s5d": s5_d[:, None, :], "wglu": s5_w_glu.astype(BF16), "bglu": s5_b_glu[:, None, :],
        "cw": ssd_conv_w, "cb": ssd_conv_b[:, None, :],
        "dtb": jnp.pad(ssd_dt_bias[:, None, :], pad_l), "alog": jnp.pad(ssd_a_log[:, None, :], pad_l),
        "dskip": jnp.repeat(ssd_d, SSD_HEADDIM, axis=-1)[:, None, :], "ng": ssd_norm_g[:, None, :],
        "wout": w_out.astype(BF16), "g2": norm2_g[:, None, :],
        "wg": w_gate.astype(BF16), "wu": w_up.astype(BF16), "wd": w_down.astype(BF16),
        "gf": final_norm_g[None, :],
    }


def _tables(seq, chunk, t0):
    C = chunk
    lg = [math.log(1.0 - 2.0 ** (-5.0 - h)) for h in range(RET_HEADS)]
    lgv = jnp.asarray(lg, F32)
    i = jnp.arange(C, dtype=F32)
    rel = i[:, None] - i[None, :]
    dec = jnp.where(rel >= 0, jnp.exp(lgv[:, None, None] * jnp.maximum(rel, 0.0)), 0.0)
    lane = jnp.arange(RET_WIDTH)
    hq = (lane % (RET_WIDTH // 2)) // (RET_DK // 2)
    hv = lane // RET_DV
    qw = jnp.exp(lgv[hq][None, :] * (i + 1.0)[:, None])
    kw = jnp.exp(lgv[hq][None, :] * (C - 1.0 - i)[:, None])
    cdc = jnp.exp(lgv[hv] * C)[None, :]
    mbd = (hq[:, None] == hv[None, :]).astype(F32)
    mavg = jnp.where(hv[:, None] == hv[None, :], 1.0 / RET_DV, 0.0).astype(BF16)
    heads = jnp.arange(RET_HEADS)[:, None, None]
    hmq = jnp.broadcast_to(hq[None, None, :] == heads, (RET_HEADS, C, RET_WIDTH)).astype(BF16)
    hmv = jnp.broadcast_to(hv[None, None, :] == heads, (RET_HEADS, C, RET_WIDTH)).astype(BF16)
    tri = (i[:, None] >= i[None, :]).astype(BF16)
    half = RET_DK // 2
    inv_freq = ROPE_BASE ** (-jnp.arange(half, dtype=F32) / half)
    pos = (t0 + jnp.arange(seq)).astype(F32)
    ang = pos[:, None] * jnp.tile(inv_freq, RET_HEADS)[None, :]
    return {"dec": dec, "qw": qw, "kw": kw, "cdc": cdc, "mbd": mbd, "mavg": mavg, "hmq": hmq, "hmv": hmv,
            "tri": tri, "cos": jnp.cos(ang), "sin": jnp.sin(ang)}


def _plan(batch, seq):
    if seq >= 512:
        return (2 if batch % 2 == 0 else 1), 512, 256
    nb = max(1, min(batch, 256 // seq))
    while batch % nb:
        nb -= 1
    return nb, seq, seq


def _trunk(x, states, prm, t0):
    batch, seq, _ = x.shape
    nb, tile, chunk = _plan(batch, seq)
    tabs = _tables(seq, chunk, t0)
    depth = prm["win"].shape[0]
    x2d = x.reshape(batch * seq, D_MODEL)
    ffn_rows = min(1024, batch * seq)
    outs = []
    for l in range(depth):
        x2d, st_out = _mixer_call(x2d, states, prm, tabs, l, batch=batch, seq=seq, nb=nb, tile=tile, chunk=chunk)
        x2d = _ffn_call(x2d, prm, l, rows=ffn_rows, final=(l == depth - 1))
        outs.append(st_out)
    stacked = [jnp.stack([o[i] for o in outs]) for i in range(5)]
    s5_shape = (depth, batch, S5_GROUPS, S5_STATE)
    return (x2d.reshape(batch, seq, D_MODEL),
            (stacked[0].reshape(s5_shape), stacked[1].reshape(s5_shape), stacked[2], stacked[3], stacked[4]))


def kernel(x_prompt, x_sample, state_s5_re, state_s5_im, state_ret, state_ssd, cache_ssd_conv, norm1_g, w_in, s5_lam_re, s5_lam_im, s5_log_dt, s5_b_re, s5_b_im, s5_c_re, s5_c_im, s5_d, s5_w_glu, s5_b_glu, ssd_conv_w, ssd_conv_b, ssd_dt_bias, ssd_a_log, ssd_d, ssd_norm_g, w_out, norm2_g, w_gate, w_up, w_down, final_norm_g):
    prm = _prepare(norm1_g, w_in, s5_lam_re, s5_lam_im, s5_log_dt, s5_b_re, s5_b_im, s5_c_re, s5_c_im,
                   s5_d, s5_w_glu, s5_b_glu, ssd_conv_w, ssd_conv_b, ssd_dt_bias, ssd_a_log, ssd_d, ssd_norm_g,
                   w_out, norm2_g, w_gate, w_up, w_down, final_norm_g)
    depth, bs = state_s5_re.shape[:2]
    past_len = 1024

    y_prompt, p_states = _trunk(x_prompt, None, prm, 0)
    s_in = (state_s5_re.reshape(depth, bs, SUBLANES, LANES), state_s5_im.reshape(depth, bs, SUBLANES, LANES),
            state_ret, state_ssd, cache_ssd_conv)
    y_sample, s_states = _trunk(x_sample, s_in, prm, past_len)
    return (y_prompt, y_sample) + p_states + s_states
```

```python
import functools
import math

import jax
import jax.numpy as jnp
from jax import lax
from jax.experimental import pallas as pl
from jax.experimental.pallas import tpu as pltpu

F32 = jnp.float32
BF16 = jnp.bfloat16

D_MODEL = 1024
EPS = 1e-6
ROPE_BASE = 10000.0
S5_WIDTH = 256
S5_GROUPS = 16
S5_GROUP_CH = 16
S5_STATE = 64
S5_LANES = S5_GROUPS * S5_STATE
RET_HEADS = 4
RET_DK = 64
RET_DV = 64
RET_WIDTH = RET_HEADS * RET_DV
SSD_WIDTH = 512
SSD_HEADDIM = 64
SSD_HEADS = 8
SSD_GROUPS = 2
SSD_DSTATE = 128
SSD_CONV = 4
SSD_CONV_DIM = SSD_WIDTH + 2 * SSD_GROUPS * SSD_DSTATE
D_FF = 2816
LANES = 128
SUBLANES = 8
MXU_DEPTH = 256

U0 = 0
Q0 = U0 + S5_WIDTH
K0 = Q0 + RET_HEADS * RET_DK
V0 = K0 + RET_HEADS * RET_DK
G0 = V0 + RET_WIDTH
Z0 = G0 + RET_WIDTH
X0 = Z0 + SSD_WIDTH
DT0 = X0 + SSD_CONV_DIM
IN_PAD = DT0 + LANES

CONV_TAIL = SSD_CONV - 1
CONV_LO = SUBLANES - CONV_TAIL

VMEM_LIMIT = 60000 * 1024


def _dot(a, b):
    return jnp.dot(a, b, preferred_element_type=F32)


def _dot_nt(a, b):
    return lax.dot_general(a, b, (((1,), (1,)), ((), ())), preferred_element_type=F32)


def _sigmoid(x):
    return 1.0 / (1.0 + jnp.exp(-x))


def _silu(x):
    return x * _sigmoid(x)


def _gelu_tanh(x):
    c = math.sqrt(2.0 / math.pi)
    return x * (0.5 * (1.0 + jnp.tanh(c * (x + 0.044715 * (x * x * x)))))


def _softplus(x):
    return jnp.maximum(x, 0.0) + jnp.log1p(jnp.exp(-jnp.abs(x)))


def _rms(x, g):
    ms = jnp.mean(x * x, axis=-1, keepdims=True)
    return x * lax.rsqrt(ms + EPS) * g


NSPLIT = 2


def _split(x):
    parts, r = [], x
    for i in range(NSPLIT):
        p = r.astype(BF16)
        parts.append(p)
        if i + 1 < NSPLIT:
            r = r - p.astype(F32)
    return jnp.concatenate(parts, axis=-1)


def _unsplit(y):
    w = y.shape[-1] // NSPLIT
    out = y[:, 0:w]
    for i in range(1, NSPLIT):
        out = out + y[:, i * w:(i + 1) * w]
    return out


def _s5_param_kernel(lr_ref, li_ref, ldt_ref, br_ref, bi_ref, ar_ref, ai_ref, bbr_ref, bbi_ref):
    depth = lr_ref.shape[0]
    for l in range(depth):
        lr = lr_ref[l:l + 1, :]
        li = li_ref[l:l + 1, :]
        dt = jnp.exp(ldt_ref[l:l + 1, :])
        mag = jnp.exp(lr * dt)
        ar = mag * jnp.cos(li * dt)
        ai = mag * jnp.sin(li * dt)
        den = lr * lr + li * li
        nr = ar - 1.0
        ni = ai
        kr = (nr * lr + ni * li) / den
        ki = (ni * lr - nr * li) / den
        br = br_ref[l]
        bi = bi_ref[l]
        bbr_ref[l] = kr * br - ki * bi
        bbi_ref[l] = kr * bi + ki * br
        ar_ref[l:l + 1, :] = ar
        ai_ref[l:l + 1, :] = ai


def _s5_params(lam_re, lam_im, log_dt, b_re, b_im):
    depth = lam_re.shape[0]
    lr = lam_re.reshape(depth, S5_LANES)
    li = lam_im.reshape(depth, S5_LANES)
    ldt = jnp.repeat(log_dt, S5_STATE, axis=-1)
    brt = jnp.transpose(b_re, (0, 3, 1, 2)).reshape(depth, S5_GROUP_CH, S5_LANES)
    bit = jnp.transpose(b_im, (0, 3, 1, 2)).reshape(depth, S5_GROUP_CH, S5_LANES)
    out_shape = (jax.ShapeDtypeStruct((depth, S5_LANES), F32),
                 jax.ShapeDtypeStruct((depth, S5_LANES), F32),
                 jax.ShapeDtypeStruct((depth, S5_GROUP_CH, S5_LANES), F32),
                 jax.ShapeDtypeStruct((depth, S5_GROUP_CH, S5_LANES), F32))
    return pl.pallas_call(_s5_param_kernel, out_shape=out_shape, name="s5_params")(lr, li, ldt, brt, bit)


def _round_robin(chains, lead=None):
    chains = list(chains)
    while chains:
        for c in list(chains):
            if next(c, StopIteration) is StopIteration:
                chains.remove(c)
        if lead is not None:
            next(lead, None)
            next(lead, None)
    if lead is not None:
        for _ in lead:
            pass


def _mixer_kernel(*refs, nb, tile, chunk, n_t, zero_init):
    (x_ref, g1_ref, win_ref, cos_ref, sin_ref,
     a8r_ref, a8i_ref, bb_ref, ct_ref, s5d_ref, wglu_ref, bglu_ref,
     dec_ref, qw_ref, kw_ref, cdc_ref, mbd_ref, mavg_ref, hmq_ref, hmv_ref, tri_ref, e3_ref,
     cw_ref, cb_ref, dtb_ref, alog_ref, dskip_ref, ng_ref, wout_ref) = refs[:29]
    refs = refs[29:]
    if not zero_init:
        s5r_in, s5i_in, ret_in, ssd_in, conv_in = refs[:5]
        refs = refs[5:]
    (xo_ref, s5r_out, s5i_out, ret_out, ssd_out, conv_out,
     proj_ref, hn_ref, mix_ref, hb_all, xp_all, ret_sc, ssd_sc, conv_sc) = refs
    unrolled = tile > chunk

    C = chunk
    nchunk = tile // C
    ic = pl.program_id(0) % n_t if n_t > 1 else 0
    half = RET_HEADS * RET_DK // 2
    hpg = SSD_HEADS // SSD_GROUPS
    gw = hpg * SSD_HEADDIM
    ret_blocks = [(h, hf, hf * half + h * (RET_DK // 2), h * RET_DV)
                  for h in range(RET_HEADS) for hf in range(2)]

    def init_states():
        if zero_init:
            s5r_out[...] = jnp.zeros_like(s5r_out)
            s5i_out[...] = jnp.zeros_like(s5i_out)
            ret_sc[...] = jnp.zeros_like(ret_sc)
            ssd_sc[...] = jnp.zeros_like(ssd_sc)
            conv_sc[...] = jnp.zeros_like(conv_sc)
            return
        s5r_out[...] = s5r_in[...]
        s5i_out[...] = s5i_in[...]
        ret_sc[...] = jnp.zeros_like(ret_sc)
        conv_sc[...] = jnp.zeros_like(conv_sc)
        for b in range(nb):
            for h, hf, r0, c0 in ret_blocks:
                ret_sc[b, r0:r0 + RET_DK // 2, c0:c0 + RET_DV] = ret_in[b, h, hf * (RET_DK // 2):(hf + 1) * (RET_DK // 2), :]
            for g in range(SSD_GROUPS):
                ssd_sc[b, g] = ssd_in[b, g * hpg:(g + 1) * hpg].reshape(gw, SSD_DSTATE).T
            conv_sc[b, CONV_LO:SUBLANES, :] = conv_in[b]

    def final_states():
        for b in range(nb):
            for h, hf, r0, c0 in ret_blocks:
                ret_out[b, h, hf * (RET_DK // 2):(hf + 1) * (RET_DK // 2), :] = ret_sc[b, r0:r0 + RET_DK // 2, c0:c0 + RET_DV]
            for g in range(SSD_GROUPS):
                ssd_out[b, g * hpg:(g + 1) * hpg] = ssd_sc[b, g].T.reshape(hpg, SSD_HEADDIM, SSD_DSTATE)
            conv_out[b] = conv_sc[b, CONV_LO:SUBLANES, :]

    if n_t > 1:
        pl.when(ic == 0)(init_states)
    else:
        init_states()

    hn_ref[...] = _rms(x_ref[...].reshape(nb * tile, D_MODEL), g1_ref[...]).astype(BF16)

    def proj_pieces(rows):
        hn = hn_ref[rows, :]
        for lo in range(0, IN_PAD, MXU_DEPTH):
            hi = min(lo + MXU_DEPTH, IN_PAD)
            proj_ref[rows, lo:hi] = _dot(hn, win_ref[:, lo:hi])
            yield

    causal = (lax.broadcasted_iota(jnp.int32, (C, C), 0) >= lax.broadcasted_iota(jnp.int32, (C, C), 1))
    lane128 = lax.broadcasted_iota(jnp.int32, (1, LANES), 1)

    def pslab(rows, lo, hi):
        return proj_ref[rows, lo:hi]

    def s5_chain(bi, ci, r0):
        rows = pl.ds(r0, C)
        hb_ref = hb_all.at[bi if unrolled else 0]
        u = pslab(rows,U0, U0 + S5_WIDTH)
        bu = _dot(u.astype(BF16), bb_ref[...])
        nslab = S5_LANES // LANES
        pitch = C + SUBLANES
        for s in range(2 * nslab):
            hb_ref[s * pitch:s * pitch + C, :] = bu[:, s * LANES:(s + 1) * LANES]
        yield
        ar = a8r_ref[...]
        ai = a8i_ref[...]
        hr = s5r_out[bi]
        hi = s5i_out[bi]
        for t in range(C):
            ld_r = pl.ds(t, nslab, stride=pitch)
            ld_i = pl.ds(nslab * pitch + t, nslab, stride=pitch)
            hr, hi = (ar * hr - ai * hi + hb_ref[ld_r, :], ar * hi + ai * hr + hb_ref[ld_i, :])
            hb_ref[ld_r, :] = hr
            hb_ref[ld_i, :] = hi
        s5r_out[bi] = hr
        s5i_out[bi] = hi
        yield
        hall = jnp.concatenate([hb_ref[s * pitch:s * pitch + C, :].astype(BF16) for s in range(2 * nslab)],
                               axis=-1)
        y = s5d_ref[...] * u + _dot_nt(hall, ct_ref[...])
        yield
        zg = _gelu_tanh(y)
        gl = _dot(zg.astype(BF16), wglu_ref[...]) + bglu_ref[...]
        yield
        mix_ref[rows, 0:S5_WIDTH] = zg * _sigmoid(gl)

    def ret_chain(bi, ci, r0):
        rows = pl.ds(r0, C)
        q1 = pslab(rows,Q0, Q0 + half)
        q2 = pslab(rows,Q0 + half, Q0 + 2 * half)
        k1 = pslab(rows,K0, K0 + half)
        k2 = pslab(rows,K0 + half, K0 + 2 * half)
        cs_ = cos_ref[pl.ds(ci * C, C), :]
        sn_ = sin_ref[pl.ds(ci * C, C), :]
        qr = jnp.concatenate([q1 * cs_ - q2 * sn_, q1 * sn_ + q2 * cs_], axis=-1)
        kr = jnp.concatenate([k1 * cs_ - k2 * sn_, k1 * sn_ + k2 * cs_], axis=-1) * (RET_DK ** -0.5)
        gate = pslab(rows,G0, G0 + RET_WIDTH)
        s_prev = ret_sc[bi]
        qb = qr.astype(BF16)
        kb = kr.astype(BF16)
        vb = pslab(rows,V0, V0 + RET_WIDTH).astype(BF16)
        o = _dot((qr * qw_ref[...]).astype(BF16), s_prev.astype(BF16))
        yield
        scs = []
        for h in range(RET_HEADS):
            scs.append(_dot_nt(qb * hmq_ref[h], kb) * dec_ref[h])
            yield
        kt = (kr * kw_ref[...]).T
        kv = _dot(kt.astype(BF16), vb)
        ret_sc[bi] = s_prev * cdc_ref[...] + kv * mbd_ref[...]
        yield
        for h in range(RET_HEADS):
            o = o + _dot(scs[h].astype(BF16), vb * hmv_ref[h])
            yield
        mavg = mavg_ref[...]
        mu = _dot(o.astype(BF16), mavg)
        yield
        dlt = o - mu
        var = _dot((dlt * dlt).astype(BF16), mavg)
        yield
        mix_ref[rows, S5_WIDTH:S5_WIDTH + RET_WIDTH] = _silu(gate) * (dlt * lax.rsqrt(var + EPS))

    def ssd_chain(bi, ci, r0):
        rows = pl.ds(r0, C)
        z = pslab(rows,Z0, Z0 + SSD_WIDTH)
        xbc = pslab(rows,X0, X0 + SSD_CONV_DIM)
        xp_ref = xp_all.at[bi if unrolled else 0]
        xp_ref[0:SUBLANES, :] = conv_sc[bi]
        xp_ref[SUBLANES:SUBLANES + C, :] = xbc
        cw = cw_ref[...]
        acc = cb_ref[...] + cw[CONV_TAIL:SSD_CONV] * xbc
        for i in range(CONV_TAIL):
            acc = acc + cw[i:i + 1] * xp_ref[CONV_LO + i:CONV_LO + i + C, :]
        conv_sc[bi] = xp_ref[C:C + SUBLANES, :]
        xc = _silu(acc)
        xs = xc[:, 0:SSD_WIDTH]
        ngl = SSD_DSTATE
        bm = xc[:, SSD_WIDTH:SSD_WIDTH + SSD_GROUPS * ngl]
        cm = xc[:, SSD_WIDTH + SSD_GROUPS * ngl:SSD_CONV_DIM]
        dt = _softplus(pslab(rows,DT0, DT0 + LANES) + dtb_ref[...])
        a_row = jnp.where(lane128 < SSD_HEADS, -jnp.exp(alog_ref[...]), 0.0)
        dta = dt * a_row
        cs_parts = _dot(tri_ref[...], _split(dta))
        yield
        cs = _unsplit(cs_parts)
        dt_e = _dot(_split(dt), e3_ref[...])
        yield
        cs_e = _dot(_split(cs), e3_ref[...])
        yield
        last_e = cs_e[C - 1:C, :]
        ecs_e = jnp.exp(cs_e)
        wend_e = jnp.exp(last_e - cs_e)
        cdec_e = jnp.exp(last_e)
        xdt = xs * dt_e
        cst = cs.T
        ys = []
        for g in range(SSD_GROUPS):
            cm_g = cm[:, g * ngl:(g + 1) * ngl].astype(BF16)
            bm_g = bm[:, g * ngl:(g + 1) * ngl]
            cbm = _dot_nt(cm_g, bm_g.astype(BF16))
            yield
            st = ssd_sc[bi, g]
            xdt_g = xdt[:, g * gw:(g + 1) * gw]
            xdt_b = xdt_g.astype(BF16)
            yg = _dot(cm_g, st.astype(BF16)) * ecs_e[:, g * gw:(g + 1) * gw]
            yield
            new = _dot(bm_g.T.astype(BF16), (xdt_g * wend_e[:, g * gw:(g + 1) * gw]).astype(BF16))
            ssd_sc[bi, g] = st * cdec_e[:, g * gw:(g + 1) * gw] + new
            yield
            for hl in range(hpg):
                h = g * hpg + hl
                seg = cs[:, h:h + 1] - cst[h:h + 1, :]
                lm = jnp.where(causal, jnp.exp(seg), 0.0)
                yg = yg + _dot((cbm * lm).astype(BF16), xdt_b * hmv_ref[hl])
                yield
            ys.append(yg)
        yss = jnp.concatenate(ys, axis=-1) + dskip_ref[...] * xs
        mix_ref[rows, S5_WIDTH + RET_WIDTH:D_MODEL] = _rms(yss * _silu(z), ng_ref[...])

    def out_chain(b, t0, n):
        mixb = mix_ref[b * tile + t0:b * tile + t0 + n, :].astype(BF16)
        for lo in range(0, D_MODEL, D_MODEL // 2):
            cols = slice(lo, lo + D_MODEL // 2)
            xo_ref[b, t0:t0 + n, cols] = x_ref[b, t0:t0 + n, cols] + _dot(mixb, wout_ref[:, cols])
            yield

    def head_groups(bi, ci, r0):
        return [s5_chain(bi, ci, r0), ret_chain(bi, ci, r0), ssd_chain(bi, ci, r0)]

    def chain_all(gens):
        for g in gens:
            yield from g

    if unrolled:
        _round_robin([proj_pieces(pl.ds(b * tile, C)) for b in range(nb)])
        prev = []
        for ci in range(nchunk):
            ahead = None
            if ci + 1 < nchunk:
                ahead = chain_all([proj_pieces(pl.ds(b * tile + (ci + 1) * C, C)) for b in range(nb)])
            groups = [g for b in range(nb) for g in head_groups(b, ci, b * tile + ci * C)]
            _round_robin(groups + prev, ahead)
            prev = [out_chain(b, ci * C, C) for b in range(nb)]
        _round_robin(prev)
    else:
        proj_ref[...] = _dot(hn_ref[...], win_ref[...])

        def seg_body(j, carry):
            _round_robin(head_groups(j, 0, pl.multiple_of(j * C, C)))
            return carry
        lax.fori_loop(0, nb, seg_body, 0)
        res = x_ref[...].reshape(nb * tile, D_MODEL) + _dot(mix_ref[...].astype(BF16), wout_ref[...])
        xo_ref[...] = res.reshape(nb, tile, D_MODEL)

    if n_t > 1:
        pl.when(ic == n_t - 1)(final_states)
    else:
        final_states()


def _const_spec(shape, l=None):
    if l is None:
        return pl.BlockSpec(shape, lambda k: (0,) * len(shape), pipeline_mode=pl.Buffered(1))
    nd = len(shape)
    return pl.BlockSpec((None,) + tuple(shape), lambda k: (l,) + (0,) * nd, pipeline_mode=pl.Buffered(1))


def _mixer_call(x2d, states, prm, tabs, l, *, batch, seq, nb, tile, chunk):
    C = chunk
    n_t = seq // tile
    total = (batch // nb) * n_t
    rows = nb * tile
    zero_init = states is None
    x3d = x2d.reshape(batch, seq, D_MODEL)
    row_spec = pl.BlockSpec((nb, tile, D_MODEL), lambda k: (k // n_t, k % n_t, 0))
    ways = nb if tile > C else 1
    rope_spec = pl.BlockSpec((tile, LANES), lambda k: (k % n_t, 0))
    st_shapes = [(SUBLANES, LANES), (SUBLANES, LANES), (RET_HEADS, RET_DK, RET_DV),
                 (SSD_HEADS, SSD_HEADDIM, SSD_DSTATE), (CONV_TAIL, SSD_CONV_DIM)]
    st_out_specs = [pl.BlockSpec((nb,) + s, lambda k, _n=len(s): (k // n_t,) + (0,) * _n) for s in st_shapes]
    st_in_specs = [pl.BlockSpec((None, nb) + s, lambda k, _n=len(s): (l, k // n_t) + (0,) * _n)
                   for s in st_shapes]
    in_specs = [
        row_spec,
        _const_spec((1, D_MODEL), l),
        _const_spec((D_MODEL, IN_PAD), l),
        rope_spec,
        rope_spec,
        _const_spec((SUBLANES, LANES), l), _const_spec((SUBLANES, LANES), l),
        _const_spec((S5_WIDTH, 2 * S5_LANES), l), _const_spec((S5_WIDTH, 2 * S5_LANES), l),
        _const_spec((1, S5_WIDTH), l), _const_spec((S5_WIDTH, S5_WIDTH), l), _const_spec((1, S5_WIDTH), l),
        _const_spec((RET_HEADS, C, C)), _const_spec((C, RET_WIDTH)), _const_spec((C, RET_WIDTH)),
        _const_spec((1, RET_WIDTH)), _const_spec((RET_WIDTH, RET_WIDTH)), _const_spec((RET_WIDTH, RET_WIDTH)),
        _const_spec((RET_HEADS, C, RET_WIDTH)), _const_spec((RET_HEADS, C, RET_WIDTH)),
        _const_spec((C, C)), _const_spec((NSPLIT * LANES, SSD_WIDTH)),
        _const_spec((SSD_CONV, SSD_CONV_DIM), l), _const_spec((1, SSD_CONV_DIM), l),
        _const_spec((1, LANES), l), _const_spec((1, LANES), l),
        _const_spec((1, SSD_WIDTH), l), _const_spec((1, SSD_WIDTH), l),
        _const_spec((D_MODEL, D_MODEL), l),
    ] + ([] if zero_init else st_in_specs)
    out_specs = [row_spec] + st_out_specs
    out_shape = ([jax.ShapeDtypeStruct(x3d.shape, F32)]
                 + [jax.ShapeDtypeStruct((batch,) + s, F32) for s in st_shapes])
    scratch = [pltpu.VMEM((rows, IN_PAD), F32), pltpu.VMEM((rows, D_MODEL), BF16),
               pltpu.VMEM((rows, D_MODEL), F32),
               pltpu.VMEM((ways, 2 * S5_LANES // LANES * (C + SUBLANES), LANES), F32),
               pltpu.VMEM((ways, C + SUBLANES, SSD_CONV_DIM), F32),
               pltpu.VMEM((nb, RET_HEADS * RET_DK, RET_WIDTH), F32),
               pltpu.VMEM((nb, SSD_GROUPS, SSD_DSTATE, SSD_WIDTH // SSD_GROUPS), F32),
               pltpu.VMEM((nb, SUBLANES, SSD_CONV_DIM), F32)]
    kern = functools.partial(_mixer_kernel, nb=nb, tile=tile, chunk=C, n_t=n_t, zero_init=zero_init)
    outs = pl.pallas_call(
        kern, grid=(total,), in_specs=in_specs, out_specs=out_specs, out_shape=out_shape,
        scratch_shapes=scratch, name=f"mixer_c{C}",
        compiler_params=pltpu.CompilerParams(dimension_semantics=("arbitrary",),
                                             vmem_limit_bytes=VMEM_LIMIT),
    )(x3d, prm["g1"], prm["win"], tabs["cos"], tabs["sin"],
      prm["a8r"], prm["a8i"], prm["bb"], prm["ct"], prm["s5d"], prm["wglu"], prm["bglu"],
      tabs["dec"], tabs["qw"], tabs["kw"], tabs["cdc"], tabs["mbd"], tabs["mavg"], tabs["hmq"], tabs["hmv"],
      tabs["tri"], tabs["e3"],
      prm["cw"], prm["cb"], prm["dtb"], prm["alog"], prm["dskip"], prm["ng"], prm["wout"],
      *(() if zero_init else states))
    return outs[0].reshape(x2d.shape), tuple(outs[1:])


def _ffn_kernel(x_ref, g2_ref, wg_ref, wu_ref, wd_ref, gf_ref, o_ref, *, final, nsub):
    split = (D_FF // (2 * MXU_DEPTH) + 1) * MXU_DEPTH
    sub = x_ref.shape[0] // nsub

    def chain(rows):
        x = x_ref[rows, :]
        hn = _rms(x, g2_ref[...]).astype(BF16)
        acc = x
        for lo, hi in ((0, split), (split, D_FF)):
            gt = _dot(hn, wg_ref[:, lo:hi])
            up = _dot(hn, wu_ref[:, lo:hi])
            yield
            acc = acc + _dot((_silu(gt) * up).astype(BF16), wd_ref[lo:hi, :])
            yield
        if final:
            acc = _rms(acc, gf_ref[...])
        o_ref[rows, :] = acc

    _round_robin([chain(pl.ds(i * sub, sub)) for i in range(nsub)])


def _ffn_call(x2d, prm, l, *, rows, final):
    n = x2d.shape[0] // rows
    row_spec = pl.BlockSpec((rows, D_MODEL), lambda i: (i, 0))

    def wspec(shape, layer=True):
        nd = len(shape)
        if layer:
            return pl.BlockSpec((None,) + shape, lambda i: (l,) + (0,) * nd, pipeline_mode=pl.Buffered(1))
        return pl.BlockSpec(shape, lambda i: (0,) * nd, pipeline_mode=pl.Buffered(1))

    return pl.pallas_call(
        functools.partial(_ffn_kernel, final=final, nsub=2), grid=(n,),
        in_specs=[row_spec, wspec((1, D_MODEL)), wspec((D_MODEL, D_FF)), wspec((D_MODEL, D_FF)),
                  wspec((D_FF, D_MODEL)), wspec((1, D_MODEL), layer=False)],
        out_specs=row_spec, out_shape=jax.ShapeDtypeStruct(x2d.shape, F32),
        name="ffn_final" if final else "ffn",
        compiler_params=pltpu.CompilerParams(dimension_semantics=("arbitrary",), vmem_limit_bytes=VMEM_LIMIT),
    )(x2d, prm["g2"], prm["wg"], prm["wu"], prm["wd"], prm["gf"])


def _reorder_w_in(w_in):
    depth = w_in.shape[0]
    qk = RET_HEADS * RET_DK

    def halves_first(w):
        w = w.reshape(depth, D_MODEL, RET_HEADS, 2, RET_DK // 2)
        return w.transpose(0, 1, 3, 2, 4).reshape(depth, D_MODEL, qk)

    wb = w_in.astype(BF16)
    parts = [wb[:, :, :S5_WIDTH], halves_first(wb[:, :, S5_WIDTH:S5_WIDTH + qk]),
             halves_first(wb[:, :, S5_WIDTH + qk:S5_WIDTH + 2 * qk]), wb[:, :, S5_WIDTH + 2 * qk:]]
    used = sum(p.shape[2] for p in parts)
    parts.append(jnp.zeros((depth, D_MODEL, IN_PAD - used), BF16))
    return jnp.concatenate(parts, axis=2)


def _prepare(norm1_g, w_in, s5_lam_re, s5_lam_im, s5_log_dt, s5_b_re, s5_b_im, s5_c_re, s5_c_im,
             s5_d, s5_w_glu, s5_b_glu, ssd_conv_w, ssd_conv_b, ssd_dt_bias, ssd_a_log, ssd_d, ssd_norm_g,
             w_out, norm2_g, w_gate, w_up, w_down, final_norm_g):
    depth = w_in.shape[0]
    a_re, a_im, bbr, bbi = _s5_params(s5_lam_re, s5_lam_im, s5_log_dt, s5_b_re, s5_b_im)
    rg = jnp.arange(S5_WIDTH)[:, None] // S5_GROUP_CH
    cg = jnp.arange(S5_LANES)[None, :] // S5_STATE
    blk = (rg == cg)[None]

    def bdiag_b(t):
        return jnp.where(blk, jnp.tile(t, (1, S5_GROUPS, 1)), 0.0)

    def bdiag_c(c):
        c2 = c.reshape(depth, S5_WIDTH, S5_STATE)
        return jnp.where(blk, jnp.tile(c2, (1, 1, S5_GROUPS)), 0.0)

    pad_l = ((0, 0), (0, 0), (0, LANES - SSD_HEADS))
    return {
        "g1": norm1_g[:, None, :], "win": _reorder_w_in(w_in),
        "a8r": a_re.reshape(depth, SUBLANES, LANES), "a8i": a_im.reshape(depth, SUBLANES, LANES),
        "bb": jnp.concatenate([bdiag_b(bbr), bdiag_b(bbi)], axis=-1).astype(BF16),
        "ct": jnp.concatenate([bdiag_c(s5_c_re), -bdiag_c(s5_c_im)], axis=-1).astype(BF16),
        "s5d": s5_d[:, None, :], "wglu": s5_w_glu.astype(BF16), "bglu": s5_b_glu[:, None, :],
        "cw": ssd_conv_w, "cb": ssd_conv_b[:, None, :],
        "dtb": jnp.pad(ssd_dt_bias[:, None, :], pad_l), "alog": jnp.pad(ssd_a_log[:, None, :], pad_l),
        "dskip": jnp.repeat(ssd_d, SSD_HEADDIM, axis=-1)[:, None, :], "ng": ssd_norm_g[:, None, :],
        "wout": w_out.astype(BF16), "g2": norm2_g[:, None, :],
        "wg": w_gate.astype(BF16), "wu": w_up.astype(BF16), "wd": w_down.astype(BF16),
        "gf": final_norm_g[None, :],
    }


def _tables(seq, chunk, t0):
    C = chunk
    lg = [math.log(1.0 - 2.0 ** (-5.0 - h)) for h in range(RET_HEADS)]
    lgv = jnp.asarray(lg, F32)
    i = jnp.arange(C, dtype=F32)
    rel = i[:, None] - i[None, :]
    dec = jnp.where(rel >= 0, jnp.exp(lgv[:, None, None] * jnp.maximum(rel, 0.0)), 0.0)
    lane = jnp.arange(RET_WIDTH)
    hq = (lane % (RET_WIDTH // 2)) // (RET_DK // 2)
    hv = lane // RET_DV
    qw = jnp.exp(lgv[hq][None, :] * (i + 1.0)[:, None])
    kw = jnp.exp(lgv[hq][None, :] * (C - 1.0 - i)[:, None])
    cdc = jnp.exp(lgv[hv] * C)[None, :]
    mbd = (hq[:, None] == hv[None, :]).astype(F32)
    mavg = jnp.where(hv[:, None] == hv[None, :], 1.0 / RET_DV, 0.0).astype(BF16)
    heads = jnp.arange(RET_HEADS)[:, None, None]
    hmq = jnp.broadcast_to(hq[None, None, :] == heads, (RET_HEADS, C, RET_WIDTH)).astype(BF16)
    hmv = jnp.broadcast_to(hv[None, None, :] == heads, (RET_HEADS, C, RET_WIDTH)).astype(BF16)
    tri = (i[:, None] >= i[None, :]).astype(BF16)
    r = jnp.arange(NSPLIT * LANES) % LANES
    e3 = (r[:, None] == (jnp.arange(SSD_WIDTH) // SSD_HEADDIM)[None, :]).astype(BF16)
    half = RET_DK // 2
    inv_freq = ROPE_BASE ** (-jnp.arange(half, dtype=F32) / half)
    pos = (t0 + jnp.arange(seq)).astype(F32)
    ang = pos[:, None] * jnp.tile(inv_freq, RET_HEADS)[None, :]
    return {"dec": dec, "qw": qw, "kw": kw, "cdc": cdc, "mbd": mbd, "mavg": mavg, "hmq": hmq, "hmv": hmv,
            "tri": tri, "e3": e3, "cos": jnp.cos(ang), "sin": jnp.sin(ang)}


def _plan(batch, seq):
    if seq >= 512:
        return (2 if batch % 2 == 0 else 1), 512, 256
    nb = max(1, min(batch, 256 // seq))
    while batch % nb:
        nb -= 1
    return nb, seq, seq


def _trunk(x, states, prm, t0):
    batch, seq, _ = x.shape
    nb, tile, chunk = _plan(batch, seq)
    tabs = _tables(seq, chunk, t0)
    depth = prm["win"].shape[0]
    x2d = x.reshape(batch * seq, D_MODEL)
    ffn_rows = min(1024, batch * seq)
    outs = []
    for l in range(depth):
        x2d, st_out = _mixer_call(x2d, states, prm, tabs, l, batch=batch, seq=seq, nb=nb, tile=tile, chunk=chunk)
        x2d = _ffn_call(x2d, prm, l, rows=ffn_rows, final=(l == depth - 1))
        outs.append(st_out)
    stacked = [jnp.stack([o[i] for o in outs]) for i in range(5)]
    s5_shape = (depth, batch, S5_GROUPS, S5_STATE)
    return (x2d.reshape(batch, seq, D_MODEL),
            (stacked[0].reshape(s5_shape), stacked[1].reshape(s5_shape), stacked[2], stacked[3], stacked[4]))


def kernel(x_prompt, x_sample, state_s5_re, state_s5_im, state_ret, state_ssd, cache_ssd_conv, norm1_g, w_in, s5_lam_re, s5_lam_im, s5_log_dt, s5_b_re, s5_b_im, s5_c_re, s5_c_im, s5_d, s5_w_glu, s5_b_glu, ssd_conv_w, ssd_conv_b, ssd_dt_bias, ssd_a_log, ssd_d, ssd_norm_g, w_out, norm2_g, w_gate, w_up, w_down, final_norm_g):
    prm = _prepare(norm1_g, w_in, s5_lam_re, s5_lam_im, s5_log_dt, s5_b_re, s5_b_im, s5_c_re, s5_c_im,
                   s5_d, s5_w_glu, s5_b_glu, ssd_conv_w, ssd_conv_b, ssd_dt_bias, ssd_a_log, ssd_d, ssd_norm_g,
                   w_out, norm2_g, w_gate, w_up, w_down, final_norm_g)
    depth, bs = state_s5_re.shape[:2]
    past_len = 1024

    y_prompt, p_states = _trunk(x_prompt, None, prm, 0)
    s_in = (state_s5_re.reshape(depth, bs, SUBLANES, LANES), state_s5_im.reshape(depth, bs, SUBLANES, LANES),
            state_ret, state_ssd, cache_ssd_conv)
    y_sample, s_states = _trunk(x_sample, s_in, prm, past_len)
    return (y_prompt, y_sample) + p_states + s_states
```
